```python
import jax, jax.numpy as jnp
from jax import lax
import numpy as np

D_MODEL = 1024
BATCH = 2
SEQ = 8192
DEPTH = 2
DEC_BATCH = 32
DEC_SEQ = 4
PAST_LEN = 16384
PAGE_SIZE = 128

N_MIXERS = 2
N_POOL_LAYERS = (DEPTH + 1) // 2
N_ATTN_LAYERS = DEPTH // 2
POOL_WINDOWS = (2, 4, 8, 16)
N_POOL_GROUPS = len(POOL_WINDOWS)
POOL_GROUP = D_MODEL // N_POOL_GROUPS
POOL_STATE = max(POOL_WINDOWS) - 1
N_HEADS = 16
HEAD_DIM = D_MODEL // N_HEADS
N_KV_HEADS = 4
GQA_GROUP = N_HEADS // N_KV_HEADS
N_IDX_HEADS = 8
IDX_DIM = 64
TOPK_MAX = 256
Q_BLOCK = 128
D_FF = -(-8 * D_MODEL // (3 * 256)) * 256
Q_COLS = N_HEADS * HEAD_DIM
KV_COLS = N_KV_HEADS * HEAD_DIM
QI_COLS = N_IDX_HEADS * IDX_DIM
IN_COLS = Q_COLS + 2 * KV_COLS + QI_COLS + IDX_DIM + N_IDX_HEADS
RMS_EPS = 1e-6

kernel_name = "pool_dsa_hybrid_decode_step"


def rmsnorm(x, g):
    xf = x.astype(jnp.float32)
    xf = xf * lax.rsqrt(jnp.mean(xf * xf, axis=-1, keepdims=True) + RMS_EPS)
    return xf.astype(x.dtype) * g


def modulation(c, w, b):
    m = jax.nn.silu(c) @ w + b
    shift, scale, gate = jnp.split(m, 3, axis=-1)
    return shift[:, None, :], scale[:, None, :], gate[:, None, :]


def swiglu(h, w_in, w_out):
    g, u = jnp.split(h @ w_in, 2, axis=-1)
    return (jax.nn.silu(g) * u) @ w_out


def pool_mix(h, prefix, pos0, w_grp, scale):
    B, T, D = h.shape
    ext = jnp.concatenate([prefix, h], axis=1)
    cs = jnp.cumsum(ext.astype(jnp.float32), axis=1)
    cs = jnp.pad(cs, ((0, 0), (1, 0), (0, 0)))
    pos = pos0 + jnp.arange(T)
    means = []
    for g, w in enumerate(POOL_WINDOWS):
        sl = slice(g * POOL_GROUP, (g + 1) * POOL_GROUP)
        hi = cs[:, POOL_STATE + 1:POOL_STATE + 1 + T, sl]
        lo = cs[:, POOL_STATE + 1 - w:POOL_STATE + 1 - w + T, sl]
        cnt = jnp.minimum(w, pos + 1).astype(jnp.float32)[None, :, None]
        means.append((hi - lo) / cnt)
    mean = jnp.stack(means, axis=2)
    diff = (mean - h.reshape(B, T, N_POOL_GROUPS, POOL_GROUP).astype(jnp.float32)).astype(h.dtype)
    y = jnp.einsum('btgc,gcd->btgd', diff, w_grp).reshape(B, T, D) * scale
    return y, ext[:, -POOL_STATE:]


def dsa_project(h, w_in):
    B, T, _ = h.shape
    p = h @ w_in
    o = np.cumsum([Q_COLS, KV_COLS, KV_COLS, QI_COLS, IDX_DIM])
    q, k, v, qi, ki, wi = jnp.split(p, o, axis=-1)
    q = q.reshape(B, T, N_KV_HEADS, GQA_GROUP, HEAD_DIM)
    k = k.reshape(B, T, N_KV_HEADS, HEAD_DIM)
    v = v.reshape(B, T, N_KV_HEADS, HEAD_DIM)
    qi = qi.reshape(B, T, N_IDX_HEADS, IDX_DIM)
    wi = wi * (N_IDX_HEADS ** -0.5 * IDX_DIM ** -0.5)
    return q, k, v, qi, ki, wi


def index_scores(qi, wi, ki):
    s = jnp.einsum('bthd,bsd->bths', qi, ki).astype(jnp.float32)
    return jnp.einsum('bths,bth->bts', jax.nn.relu(s), wi.astype(jnp.float32))


def sparse_attend(q, k_sel, v_sel, valid):
    B, T = q.shape[:2]
    s = jnp.einsum('btngd,btknd->btngk', q, k_sel).astype(jnp.float32) * (HEAD_DIM ** -0.5)
    s = jnp.where(valid[:, :, None, None, :], s, -jnp.inf)
    p = jax.nn.softmax(s, axis=-1).astype(v_sel.dtype)
    o = jnp.einsum('btngk,btknd->btngd', p, v_sel)
    return o.reshape(B, T, N_HEADS * HEAD_DIM)


def dsa_prompt(h, w_in, w_o):
    B, S, _ = h.shape
    q, k, v, qi, ki, wi = dsa_project(h, w_in)
    n_sel = min(TOPK_MAX, S // 4)
    nb = S // Q_BLOCK
    b_idx = jnp.arange(B)[:, None, None]
    key_pos = jnp.arange(S)

    def to_blocks(a):
        return jnp.moveaxis(a.reshape((B, nb, Q_BLOCK) + a.shape[2:]), 1, 0)

    def block(args):
        qb, qib, wib, t0 = args
        tq = t0 + jnp.arange(Q_BLOCK)
        sc = index_scores(qib, wib, ki)
        sc = jnp.where((key_pos[None, :] <= tq[:, None])[None], sc, -jnp.inf)
        _, sel = lax.top_k(sc, n_sel)
        valid = sel <= tq[None, :, None]
        return sparse_attend(qb, k[b_idx, sel], v[b_idx, sel], valid)

    o = lax.map(block, (to_blocks(q), to_blocks(qi), to_blocks(wi), jnp.arange(nb) * Q_BLOCK))
    o = jnp.moveaxis(o, 0, 1).reshape(B, S, N_HEADS * HEAD_DIM)
    return o @ w_o, (k, v, ki)


def dsa_sample(h, cache_k, cache_v, cache_ki, page_table, w_in, w_o):
    B, T, _ = h.shape
    q, k, v, qi, ki, wi = dsa_project(h, w_in)
    past = page_table.shape[1] * PAGE_SIZE
    L = past + T
    n_sel = min(TOPK_MAX, L // 4)
    ki_past = cache_ki[page_table].reshape(B, past, IDX_DIM)
    ki_all = jnp.concatenate([ki_past, ki], axis=1)
    tq = past + jnp.arange(T)
    sc = index_scores(qi, wi, ki_all)
    sc = jnp.where((jnp.arange(L)[None, :] <= tq[:, None])[None], sc, -jnp.inf)
    _, sel = lax.top_k(sc, n_sel)
    valid = sel <= tq[None, :, None]
    b_idx = jnp.arange(B)[:, None, None]
    pidx = jnp.clip(sel, 0, past - 1)
    phys = page_table[b_idx, pidx // PAGE_SIZE] * PAGE_SIZE + pidx % PAGE_SIZE
    nidx = jnp.clip(sel - past, 0, T - 1)
    is_past = (sel < past)[..., None, None]
    flat_k = cache_k.reshape(-1, N_KV_HEADS, HEAD_DIM)
    flat_v = cache_v.reshape(-1, N_KV_HEADS, HEAD_DIM)
    k_sel = jnp.where(is_past, flat_k[phys], k[b_idx, nidx])
    v_sel = jnp.where(is_past, flat_v[phys], v[b_idx, nidx])
    o = sparse_attend(q, k_sel, v_sel, valid)
    return o @ w_o, (k, v, ki)


def run_trunk(x, c, pos0, pool_prefix_fn, attn_fn, ada_w, ada_b, norm_g, pool_w, pool_scale,
              ffn_w_in, ffn_w_out, final_g):
    pool_new, k_new, v_new, ki_new = [], [], [], []
    for i in range(DEPTH):
        j = i // N_MIXERS
        sh, sc, gt = modulation(c, ada_w[i, 0], ada_b[i, 0])
        h = rmsnorm(x, norm_g[i, 0]) * (1 + sc) + sh
        if i % N_MIXERS == 0:
            out, st = pool_mix(h, pool_prefix_fn(j), pos0, pool_w[j], pool_scale[j])
            pool_new.append(st)
        else:
            out, (kk, vv, kki) = attn_fn(j, h)
            k_new.append(kk)
            v_new.append(vv)
            ki_new.append(kki)
        x = x + gt * out
        sh, sc, gt = modulation(c, ada_w[i, 1], ada_b[i, 1])
        h = rmsnorm(x, norm_g[i, 1]) * (1 + sc) + sh
        x = x + gt * swiglu(h, ffn_w_in[i], ffn_w_out[i])
    y = rmsnorm(x, final_g)
    return y, jnp.stack(pool_new), jnp.stack(k_new), jnp.stack(v_new), jnp.stack(ki_new)


def setup_inputs(seed: int = 0) -> dict:
    key = jax.random.key(seed)
    ks = jax.random.split(key, 24)
    n_pages = PAST_LEN // PAGE_SIZE
    n_used = DEC_BATCH * n_pages
    n_phys = n_used + n_used // 4
    f32 = jnp.float32
    nrm = lambda k, s, sc: jax.random.normal(k, s, f32) * sc
    page_table = jax.random.permutation(ks[0], n_phys)[:n_used].reshape(DEC_BATCH, n_pages).astype(jnp.int32)
    return {
        "x_prompt": nrm(ks[1], (BATCH, SEQ, D_MODEL), 1.0),
        "x_sample": nrm(ks[2], (DEC_BATCH, DEC_SEQ, D_MODEL), 1.0),
        "c_prompt": nrm(ks[3], (BATCH, D_MODEL), 1.0),
        "c_sample": nrm(ks[4], (DEC_BATCH, D_MODEL), 1.0),
        "state_pool": nrm(ks[5], (N_POOL_LAYERS, DEC_BATCH, POOL_STATE, D_MODEL), 1.0),
        "cache_k": nrm(ks[6], (N_ATTN_LAYERS, n_phys, PAGE_SIZE, N_KV_HEADS, HEAD_DIM), 1.0),
        "cache_v": nrm(ks[7], (N_ATTN_LAYERS, n_phys, PAGE_SIZE, N_KV_HEADS, HEAD_DIM), 1.0),
        "cache_kidx": nrm(ks[8], (N_ATTN_LAYERS, n_phys, PAGE_SIZE, IDX_DIM), 1.0),
        "page_table": page_table,
        "ada_w": nrm(ks[9], (DEPTH, 2, D_MODEL, 3 * D_MODEL), D_MODEL ** -0.5),
        "ada_b": nrm(ks[10], (DEPTH, 2, 3 * D_MODEL), 0.01),
        "norm_g": 1.0 + nrm(ks[11], (DEPTH, 2, D_MODEL), 0.05),
        "pool_w": nrm(ks[12], (N_POOL_LAYERS, N_POOL_GROUPS, POOL_GROUP, POOL_GROUP), POOL_GROUP ** -0.5),
        "pool_scale": 1.0 + nrm(ks[13], (N_POOL_LAYERS, D_MODEL), 0.1),
        "attn_w_in": nrm(ks[14], (N_ATTN_LAYERS, D_MODEL, IN_COLS), D_MODEL ** -0.5),
        "attn_w_o": nrm(ks[15], (N_ATTN_LAYERS, N_HEADS * HEAD_DIM, D_MODEL), (N_HEADS * HEAD_DIM) ** -0.5),
        "ffn_w_in": nrm(ks[16], (DEPTH, D_MODEL, 2 * D_FF), D_MODEL ** -0.5),
        "ffn_w_out": nrm(ks[17], (DEPTH, D_FF, D_MODEL), D_FF ** -0.5),
        "final_g": 1.0 + nrm(ks[18], (D_MODEL,), 0.05),
    }


def reference(x_prompt, x_sample, c_prompt, c_sample, state_pool, cache_k, cache_v, cache_kidx, page_table,
              ada_w, ada_b, norm_g, pool_w, pool_scale, attn_w_in, attn_w_o, ffn_w_in, ffn_w_out, final_g):
    zero_prefix = jnp.zeros((x_prompt.shape[0], POOL_STATE, x_prompt.shape[2]), x_prompt.dtype)

    def prompt_attn(j, h):
        return dsa_prompt(h, attn_w_in[j], attn_w_o[j])

    def sample_attn(j, h):
        return dsa_sample(h, cache_k[j], cache_v[j], cache_kidx[j], page_table, attn_w_in[j], attn_w_o[j])

    past = page_table.shape[1] * PAGE_SIZE
    y_prompt, pool_p, k_p, v_p, ki_p = run_trunk(
        x_prompt, c_prompt, 0, lambda j: zero_prefix, prompt_attn,
        ada_w, ada_b, norm_g, pool_w, pool_scale, ffn_w_in, ffn_w_out, final_g)
    y_sample, pool_s, k_s, v_s, ki_s = run_trunk(
        x_sample, c_sample, past, lambda j: state_pool[j], sample_attn,
        ada_w, ada_b, norm_g, pool_w, pool_scale, ffn_w_in, ffn_w_out, final_g)
    return (y_prompt, y_sample, pool_p, k_p, v_p, ki_p, pool_s, k_s, v_s, ki_s)
```

```python
import functools

import jax
import jax.numpy as jnp
from jax import lax
from jax.experimental import pallas as pl
from jax.experimental.pallas import tpu as pltpu

F32 = jnp.float32
BF16 = jnp.bfloat16

N_HEADS = 16
N_KV_HEADS = 4
GQA_GROUP = N_HEADS // N_KV_HEADS
HEAD_DIM = 64
N_IDX_HEADS = 8
IDX_DIM = 64
TOPK_MAX = 256
POOL_WINDOWS = (2, 4, 8, 16)
POOL_STATE = max(POOL_WINDOWS) - 1
RMS_EPS = 1e-6

LANES = 128
SUBLANES = 8
VMEM_LIMIT_BYTES = 56 * 1024 * 1024
MASKED_SCORE = -1e30
F32_LOWEST = -3.4028234663852886e38
KEY_LOWEST = -2139095040
KEY_POS_INF = 0x7F800000
SELECT_MAX_ITERS = 34
POOL_HALO = 24


def _cparams(*semantics):
    return pltpu.CompilerParams(dimension_semantics=semantics,
                                vmem_limit_bytes=VMEM_LIMIT_BYTES)


def _silu(x):
    return x * jax.nn.sigmoid(x)


def _normmod(x, g, sc, sh):
    ms = jnp.mean(x * x, axis=-1, keepdims=True)
    return x * lax.rsqrt(ms + RMS_EPS) * g * (1.0 + sc) + sh


def _resident(shape):
    nd = len(shape)
    return pl.BlockSpec(shape, lambda *_: (0,) * nd)


def _mod_kernel(c_ref, w_ref, b_ref, o_ref):
    a = _silu(c_ref[...]).astype(BF16)
    o_ref[0] = jnp.dot(a, w_ref[0].astype(BF16), preferred_element_type=F32) + b_ref[0]


def _modulation(c_all, ada_w, ada_b):
    n_l, d, d3 = ada_w.shape
    bc = c_all.shape[0]
    tn = d3 // 2
    return pl.pallas_call(
        _mod_kernel,
        grid=(n_l, d3 // tn),
        in_specs=[pl.BlockSpec((bc, d), lambda l, j: (0, 0)),
                  pl.BlockSpec((1, d, tn), lambda l, j: (l, 0, j)),
                  pl.BlockSpec((1, 1, tn), lambda l, j: (l, 0, j))],
        out_specs=pl.BlockSpec((1, bc, tn), lambda l, j: (l, 0, j)),
        out_shape=jax.ShapeDtypeStruct((n_l, bc, d3), F32),
        compiler_params=_cparams("parallel", "parallel"),
        name="modulation",
    )(c_all, ada_w, ada_b)


def _pool_kernel(x_ref, pre_ref, sh_ref, sc_ref, gt_ref, g_ref, w_ref, ps_ref,
                 o_ref, st_ref, e_ref, b1_ref, b2_ref, b3_ref, *, tr, t_valid, pos0):
    t = pl.program_id(1)
    d = x_ref.shape[-1]
    grp = d // len(POOL_WINDOWS)
    top = tr + POOL_HALO

    @pl.when(t == 0)
    def _():
        zeros = jnp.zeros((SUBLANES, d), F32)
        e_ref[0:SUBLANES, :] = zeros
        b1_ref[0:SUBLANES, :] = zeros
        b2_ref[0:SUBLANES, :] = zeros
        e_ref[SUBLANES:POOL_HALO, :] = pre_ref[0]

    x = x_ref[0]
    h = _normmod(x, g_ref[...], sc_ref[0], sh_ref[0])
    e_ref[POOL_HALO:top, :] = h
    b1_ref[SUBLANES:top, :] = e_ref[SUBLANES:top, :] + e_ref[SUBLANES - 1:top - 1, :]
    b2_ref[SUBLANES:top, grp:] = b1_ref[SUBLANES:top, grp:] + b1_ref[SUBLANES - 2:top - 2, grp:]
    b3_ref[SUBLANES:top, 2 * grp:] = (b2_ref[SUBLANES:top, 2 * grp:]
                                       + b2_ref[SUBLANES - 4:top - 4, 2 * grp:])
    sums = (b1_ref[POOL_HALO:top, 0:grp],
            b2_ref[POOL_HALO:top, grp:2 * grp],
            b3_ref[POOL_HALO:top, 2 * grp:3 * grp],
            b3_ref[POOL_HALO:top, 3 * grp:] + b3_ref[POOL_HALO - 8:top - 8, 3 * grp:])

    pos = pos0 + t * tr + lax.broadcasted_iota(jnp.int32, (tr, 1), 0)
    ys = []
    for gi, w in enumerate(POOL_WINDOWS):
        inv_cnt = 1.0 / jnp.minimum(w, pos + 1).astype(F32)
        diff = sums[gi] * inv_cnt - h[:, gi * grp:(gi + 1) * grp]
        ys.append(jnp.dot(diff.astype(BF16), w_ref[gi], preferred_element_type=F32))
    y = jnp.concatenate(ys, axis=-1) * ps_ref[...]
    o_ref[0] = x + gt_ref[0] * y

    st_ref[0] = e_ref[POOL_HALO + t_valid - POOL_STATE:POOL_HALO + t_valid, :]
    e_ref[SUBLANES:POOL_HALO, :] = e_ref[tr + SUBLANES:top, :]


def _pool_layer(x, prefix, sh, sc, gt, g, w_bf, ps, *, tr, t_valid, pos0):
    b, r, d = x.shape
    grp = d // len(POOL_WINDOWS)
    nt = r // tr
    row = pl.BlockSpec((1, tr, d), lambda i, j: (i, j, 0))
    per_b = pl.BlockSpec((1, 1, d), lambda i, j: (i, 0, 0))
    vec = pl.BlockSpec((1, d), lambda i, j: (0, 0))
    rows = tr + POOL_HALO
    return pl.pallas_call(
        functools.partial(_pool_kernel, tr=tr, t_valid=t_valid, pos0=pos0),
        grid=(b, nt),
        in_specs=[row,
                  pl.BlockSpec((1, 16, d), lambda i, j: (i, 0, 0)),
                  per_b, per_b, per_b, vec,
                  _resident((len(POOL_WINDOWS), grp, grp)),
                  vec],
        out_specs=[row, pl.BlockSpec((1, POOL_STATE, d), lambda i, j: (i, 0, 0))],
        out_shape=[jax.ShapeDtypeStruct((b, r, d), F32),
                   jax.ShapeDtypeStruct((b, POOL_STATE, d), F32)],
        scratch_shapes=[pltpu.VMEM((rows, d), F32)] * 4,
        compiler_params=_cparams("parallel", "arbitrary"),
        name="pool_layer",
    )(x, prefix, sh, sc, gt, g, w_bf, ps)


def _ffn_kernel(*refs, has_attn, has_final, n_chunks):
    it = iter(refs)
    x_ref = next(it)
    if has_attn:
        att_ref, gta_ref, wo_ref = next(it), next(it), next(it)
    sh_ref, sc_ref, gt_ref, g_ref, win_ref, wout_ref = (next(it) for _ in range(6))
    if has_final:
        fg_ref = next(it)
    o_ref = next(it)

    x = x_ref[0]
    if has_attn:
        x = x + gta_ref[0] * jnp.dot(att_ref[0], wo_ref[...], preferred_element_type=F32)
    h = _normmod(x, g_ref[...], sc_ref[0], sh_ref[0]).astype(BF16)
    f = wout_ref.shape[0]
    fc = f // n_chunks
    acc = jnp.zeros(x.shape, F32)
    for c in range(n_chunks):
        gate = jnp.dot(h, win_ref[:, c * fc:(c + 1) * fc], preferred_element_type=F32)
        up = jnp.dot(h, win_ref[:, f + c * fc:f + (c + 1) * fc], preferred_element_type=F32)
        act = (_silu(gate) * up).astype(BF16)
        acc = acc + jnp.dot(act, wout_ref[c * fc:(c + 1) * fc, :], preferred_element_type=F32)
    y = x + gt_ref[0] * acc
    if has_final:
        ms = jnp.mean(y * y, axis=-1, keepdims=True)
        y = y * lax.rsqrt(ms + RMS_EPS) * fg_ref[...]
    o_ref[0] = y


def _mod_spec(m, tr):
    d = m.shape[-1]
    if m.shape[1] == 1:
        return pl.BlockSpec((1, 1, d), lambda i, j: (i, 0, 0))
    return pl.BlockSpec((1, tr, d), lambda i, j: (i, j, 0))


def _ffn_layer(x, sh, sc, gt, g, win_bf, wout_bf, *, tr, attn=None, final_g=None):
    b, r, d = x.shape
    f = wout_bf.shape[0]
    row = pl.BlockSpec((1, tr, d), lambda i, j: (i, j, 0))
    vec = pl.BlockSpec((1, d), lambda i, j: (0, 0))
    args, specs = [x], [row]
    if attn is not None:
        att, gta, wo_bf = attn
        args += [att, gta, wo_bf]
        specs += [pl.BlockSpec((1, tr, att.shape[-1]), lambda i, j: (i, j, 0)),
                  _mod_spec(gta, tr), _resident(wo_bf.shape)]
    args += [sh, sc, gt, g, win_bf, wout_bf]
    specs += [_mod_spec(sh, tr), _mod_spec(sc, tr), _mod_spec(gt, tr), vec,
              _resident(win_bf.shape), _resident(wout_bf.shape)]
    if final_g is not None:
        args.append(final_g)
        specs.append(vec)
    n_chunks = 2 if f % (2 * LANES) == 0 else 1
    return pl.pallas_call(
        functools.partial(_ffn_kernel, has_attn=attn is not None,
                          has_final=final_g is not None, n_chunks=n_chunks),
        grid=(b, r // tr),
        in_specs=specs,
        out_specs=row,
        out_shape=jax.ShapeDtypeStruct((b, r, d), F32),
        compiler_params=_cparams("parallel", "parallel"),
        name="ffn_layer",
    )(*args)


def _proj_kernel(x_ref, sh_ref, sc_ref, g_ref, w_ref,
                 k_ref, v_ref, ki_ref, wi_ref, q_ref, qi_ref, kh_ref, va_ref, kib_ref):
    tr = x_ref.shape[1]
    kv = N_KV_HEADS * HEAD_DIM
    o_k = N_HEADS * HEAD_DIM
    o_v = o_k + kv
    o_qi = o_v + kv
    o_ki = o_qi + N_IDX_HEADS * IDX_DIM
    o_wi = o_ki + IDX_DIM
    h = _normmod(x_ref[0], g_ref[...], sc_ref[0], sh_ref[0]).astype(BF16)
    p = jnp.dot(h, w_ref[...], preferred_element_type=F32)
    k_ref[0] = p[:, o_k:o_v]
    v_ref[0] = p[:, o_v:o_qi]
    ki = p[:, o_ki:o_wi]
    ki_ref[0] = ki
    kib_ref[0] = ki.astype(BF16)
    wi_ref[0] = p[:, o_wi:o_wi + N_IDX_HEADS] * (N_IDX_HEADS ** -0.5 * IDX_DIM ** -0.5)
    for hd in range(N_HEADS):
        q_ref[0, hd] = (p[:, hd * HEAD_DIM:(hd + 1) * HEAD_DIM] * HEAD_DIM ** -0.5).astype(BF16)
    for hd in range(N_IDX_HEADS):
        qi_ref[0, hd] = p[:, o_qi + hd * IDX_DIM:o_qi + (hd + 1) * IDX_DIM].astype(BF16)
    ones_pad = (lax.broadcasted_iota(jnp.int32, (tr, LANES - HEAD_DIM), 1) == 0).astype(F32)
    for n in range(N_KV_HEADS):
        kh_ref[0, n] = p[:, o_k + n * HEAD_DIM:o_k + (n + 1) * HEAD_DIM].astype(BF16)
        vn = p[:, o_v + n * HEAD_DIM:o_v + (n + 1) * HEAD_DIM]
        va_ref[0, n] = jnp.concatenate([vn, ones_pad], axis=-1).astype(BF16)


def _dsa_project(x, sh, sc, g, w_bf, *, tr):
    b, r, d = x.shape
    kv = N_KV_HEADS * HEAD_DIM
    row = lambda n: pl.BlockSpec((1, tr, n), lambda i, j: (i, j, 0))
    heads = lambda nh, n: pl.BlockSpec((1, nh, tr, n), lambda i, j: (i, 0, j, 0))
    sds = jax.ShapeDtypeStruct
    return pl.pallas_call(
        _proj_kernel,
        grid=(b, r // tr),
        in_specs=[row(d), _mod_spec(sh, tr), _mod_spec(sc, tr),
                  pl.BlockSpec((1, d), lambda i, j: (0, 0)), _resident(w_bf.shape)],
        out_specs=[row(kv), row(kv), row(IDX_DIM), row(N_IDX_HEADS),
                   heads(N_HEADS, HEAD_DIM), heads(N_IDX_HEADS, IDX_DIM),
                   heads(N_KV_HEADS, HEAD_DIM), heads(N_KV_HEADS, LANES), row(IDX_DIM)],
        out_shape=[sds((b, r, kv), F32), sds((b, r, kv), F32), sds((b, r, IDX_DIM), F32),
                   sds((b, r, N_IDX_HEADS), F32),
                   sds((b, N_HEADS, r, HEAD_DIM), BF16), sds((b, N_IDX_HEADS, r, IDX_DIM), BF16),
                   sds((b, N_KV_HEADS, r, HEAD_DIM), BF16), sds((b, N_KV_HEADS, r, LANES), BF16),
                   sds((b, r, IDX_DIM), BF16)],
        compiler_params=_cparams("parallel", "parallel"),
        name="dsa_project",
    )(x, sh, sc, g, w_bf)


def _key_to_f32(k):
    return lax.bitcast_convert_type(jnp.where(k >= 0, k, k ^ 0x7FFFFFFF), F32)


def _lane_fold(m):
    part = m[:, 0:LANES]
    for j in range(1, m.shape[1] // LANES):
        part = part + m[:, j * LANES:(j + 1) * LANES]
    return part


def _count(i_ref, n_chunks, cw, pred):
    r = i_ref.shape[0]

    def body(c, acc):
        c0 = pl.multiple_of(c * cw, cw)
        return acc + _lane_fold(jnp.where(pred(i_ref[:, pl.ds(c0, cw)], c0), 1.0, 0.0))

    acc = lax.fori_loop(0, n_chunks, body, jnp.zeros((r, LANES), F32))
    return jnp.sum(acc, axis=1, keepdims=True)


def _select_threshold(i_ref, n_chunks, cw, n_valid, k_sel):
    r = i_ref.shape[0]
    kf = float(k_sel)

    def cond(st):
        it, n_open = st[0], st[1]
        return jnp.logical_and(it < SELECT_MAX_ITERS, n_open > 0)

    def body(st):
        it, _, lo, hi, c_lo, c_hi = st
        mid = (lo & hi) + ((lo ^ hi) >> 1)
        active = jnp.logical_and(c_lo > kf, mid != lo)
        thr = _key_to_f32(mid)
        c = _count(i_ref, n_chunks, cw, lambda x, c0: x >= thr)
        ge = c >= kf
        up = jnp.logical_and(active, ge)
        dn = jnp.logical_and(active, jnp.logical_not(ge))
        lo = jnp.where(up, mid, lo)
        c_lo = jnp.where(up, c, c_lo)
        hi = jnp.where(dn, mid, hi)
        c_hi = jnp.where(dn, c, c_hi)
        nxt = (lo & hi) + ((lo ^ hi) >> 1)
        still = jnp.logical_and(c_lo > kf, nxt != lo)
        n_open = jnp.sum(jnp.where(still, 1.0, 0.0))
        return it + 1, n_open, lo, hi, c_lo, c_hi

    lo0 = jnp.full((r, 1), KEY_LOWEST, jnp.int32)
    hi0 = jnp.full((r, 1), KEY_POS_INF, jnp.int32)
    n_open0 = jnp.sum(jnp.where(n_valid > kf, 1.0, 0.0))
    st = lax.while_loop(cond, body, (jnp.int32(0), n_open0, lo0, hi0,
                                     n_valid, jnp.zeros((r, 1), F32)))
    _, _, lo, _, c_lo, c_hi = st
    thr = _key_to_f32(lo)
    tied = c_lo > kf
    need = kf - c_hi

    @pl.when(jnp.sum(jnp.where(tied, 1.0, 0.0)) > 0.0)
    def _():
        n_cols = n_chunks * cw

        def col_ids(c0, shape):
            return c0 + lax.broadcasted_iota(jnp.int32, shape, 1)

        def jbody(_, st):
            jlo, jhi = st
            jm = (jlo + jhi) >> 1
            c = _count(i_ref, n_chunks, cw,
                       lambda x, c0: jnp.logical_and(x == thr, col_ids(c0, x.shape) <= jm))
            ok = c >= need
            return jnp.where(ok, jlo, jm), jnp.where(ok, jm, jhi)

        n_it = max(1, (i_ref.shape[1] - 1).bit_length())
        _, jcut = lax.fori_loop(0, n_it, jbody,
                                (jnp.full((r, 1), -1, jnp.int32),
                                 jnp.zeros((r, 1), jnp.int32) + (n_cols - 1)))

        def drop(c, carry):
            c0 = pl.multiple_of(c * cw, cw)
            x = i_ref[:, pl.ds(c0, cw)]
            kill = jnp.logical_and(jnp.logical_and(tied, x == thr), col_ids(c0, x.shape) > jcut)
            i_ref[:, pl.ds(c0, cw)] = jnp.where(kill, -jnp.inf, x)
            return carry

        lax.fori_loop(0, n_chunks, drop, 0)

    return thr


def _dsa_prompt_kernel(qi_ref, wi_ref, q_ref, ki_ref, k_ref, v_ref, o_ref,
                       i_ref, m_ref, acc_ref, *, tq, tk, k_sel):
    q0 = pl.program_id(1) * tq
    n_tiles = (q0 + tq + tk - 1) // tk
    rows = q0 + lax.broadcasted_iota(jnp.int32, (tq, 1), 0)
    nt = (((1,), (1,)), ((), ()))
    wi = wi_ref[0]

    def scores(kt, carry):
        c0 = pl.multiple_of(kt * tk, tk)
        ki = ki_ref[0, pl.ds(c0, tk), :]
        acc = jnp.zeros((tq, tk), F32)
        for hd in range(N_IDX_HEADS):
            s = lax.dot_general(qi_ref[0, hd], ki, nt, preferred_element_type=F32)
            acc = acc + jnp.maximum(s, 0.0) * wi[:, hd:hd + 1]
        cols = c0 + lax.broadcasted_iota(jnp.int32, (1, tk), 1)
        i_ref[:, pl.ds(c0, tk)] = jnp.where(cols <= rows, acc, -jnp.inf)
        return carry

    lax.fori_loop(0, n_tiles, scores, 0)

    thr = _select_threshold(i_ref, n_tiles, tk, (rows + 1).astype(F32), k_sel)

    m_ref[...] = jnp.full(m_ref.shape, MASKED_SCORE, F32)
    acc_ref[...] = jnp.zeros(acc_ref.shape, F32)

    def attend(kt, carry):
        c0 = pl.multiple_of(kt * tk, tk)
        keep = i_ref[:, pl.ds(c0, tk)] >= thr
        for n in range(N_KV_HEADS):
            qn = q_ref[0, n * GQA_GROUP:(n + 1) * GQA_GROUP].reshape(GQA_GROUP * tq, HEAD_DIM)
            s = lax.dot_general(qn, k_ref[0, n, pl.ds(c0, tk), :], nt, preferred_element_type=F32)
            s = jnp.where(keep[None], s.reshape(GQA_GROUP, tq, tk), MASKED_SCORE)
            s = s.reshape(GQA_GROUP * tq, tk)
            m_old = m_ref[n]
            m_new = jnp.maximum(m_old, jnp.max(s, axis=1, keepdims=True))
            p = jnp.exp(s - m_new).astype(BF16)
            pv = jnp.dot(p, v_ref[0, n, pl.ds(c0, tk), :], preferred_element_type=F32)
            acc_ref[n] = jnp.exp(m_old - m_new) * acc_ref[n] + pv
            m_ref[n] = m_new
        return carry

    lax.fori_loop(0, n_tiles, attend, 0)

    for n in range(N_KV_HEADS):
        a = acc_ref[n]
        o = a[:, 0:HEAD_DIM] / a[:, HEAD_DIM:HEAD_DIM + 1]
        for g in range(GQA_GROUP):
            hd = n * GQA_GROUP + g
            o_ref[0, :, hd * HEAD_DIM:(hd + 1) * HEAD_DIM] = o[g * tq:(g + 1) * tq].astype(o_ref.dtype)


def _dsa_prompt(qi_hm, wi, q_hm, ki_bf, k_hm, v_aug, *, tq, tk):
    b, _, s, _ = q_hm.shape
    k_sel = min(TOPK_MAX, s // 4)
    return pl.pallas_call(
        functools.partial(_dsa_prompt_kernel, tq=tq, tk=tk, k_sel=k_sel),
        grid=(b, s // tq),
        in_specs=[pl.BlockSpec((1, N_IDX_HEADS, tq, IDX_DIM), lambda i, j: (i, 0, j, 0)),
                  pl.BlockSpec((1, tq, N_IDX_HEADS), lambda i, j: (i, j, 0)),
                  pl.BlockSpec((1, N_HEADS, tq, HEAD_DIM), lambda i, j: (i, 0, j, 0)),
                  pl.BlockSpec((1, s, IDX_DIM), lambda i, j: (i, 0, 0)),
                  pl.BlockSpec((1, N_KV_HEADS, s, HEAD_DIM), lambda i, j: (i, 0, 0, 0)),
                  pl.BlockSpec((1, N_KV_HEADS, s, LANES), lambda i, j: (i, 0, 0, 0))],
        out_specs=pl.BlockSpec((1, tq, N_HEADS * HEAD_DIM), lambda i, j: (i, j, 0)),
        out_shape=jax.ShapeDtypeStruct((b, s, N_HEADS * HEAD_DIM), BF16),
        scratch_shapes=[pltpu.VMEM((tq, s), F32),
                        pltpu.VMEM((N_KV_HEADS, GQA_GROUP * tq, 1), F32),
                        pltpu.VMEM((N_KV_HEADS, GQA_GROUP * tq, LANES), F32)],
        compiler_params=_cparams("parallel", "arbitrary"),
        name="dsa_prompt",
    )(qi_hm, wi, q_hm, ki_bf, k_hm, v_aug)


def _page_specs(n, page_shape, pc):
    def spec(j):
        return pl.BlockSpec((1,) + page_shape,
                            lambda b, c, pt: (pt[b, jnp.minimum(c * pc + j, n - 1)], 0, 0))
    return [spec(j) for j in range(pc)]


def _sample_scores_kernel(pt_ref, qi_ref, wi_ref, kin_ref, *rest, pc, t_dec):
    pages, (ip_ref, in_ref) = rest[:pc], rest[pc:]
    c = pl.program_id(1)
    nt = (((1,), (1,)), ((), ()))
    qi = qi_ref[0]
    wi = wi_ref[0]
    page = pages[0].shape[1]

    def score(ki):
        s = lax.dot_general(qi, ki, nt, preferred_element_type=F32)
        s = jnp.maximum(s, 0.0) * wi
        return jnp.sum(s.reshape(t_dec, N_IDX_HEADS, s.shape[-1]), axis=1)

    for j in range(pc):
        ip_ref[0, :, j * page:(j + 1) * page] = score(pages[j][0].astype(BF16))

    @pl.when(c == pl.num_programs(1) - 1)
    def _():
        s = score(kin_ref[0])
        col = lax.broadcasted_iota(jnp.int32, s.shape, 1)
        row = lax.broadcasted_iota(jnp.int32, s.shape, 0)
        in_ref[0] = jnp.where(col <= row, s, -jnp.inf)


def _sample_scores(page_table, qi_s, wi_s, ki_new, cache_ki, *, pc, t_dec):
    bd, n_pages = page_table.shape
    page = cache_ki.shape[1]
    n_ch = n_pages // pc
    rows = qi_s.shape[1]
    grid_spec = pltpu.PrefetchScalarGridSpec(
        num_scalar_prefetch=1,
        grid=(bd, n_ch),
        in_specs=[pl.BlockSpec((1, rows, IDX_DIM), lambda b, c, pt: (b, 0, 0)),
                  pl.BlockSpec((1, rows, 1), lambda b, c, pt: (b, 0, 0)),
                  pl.BlockSpec((1, LANES, IDX_DIM), lambda b, c, pt: (b, 0, 0))]
                 + _page_specs(n_pages, (page, IDX_DIM), pc),
        out_specs=[pl.BlockSpec((1, t_dec, pc * page), lambda b, c, pt: (b, 0, c)),
                   pl.BlockSpec((1, t_dec, LANES), lambda b, c, pt: (b, 0, 0))],
    )
    return pl.pallas_call(
        functools.partial(_sample_scores_kernel, pc=pc, t_dec=t_dec),
        grid_spec=grid_spec,
        out_shape=[jax.ShapeDtypeStruct((bd, t_dec, n_pages * page), F32),
                   jax.ShapeDtypeStruct((bd, t_dec, LANES), F32)],
        compiler_params=_cparams("parallel", "arbitrary"),
        name="sample_scores",
    )(page_table, qi_s, wi_s, ki_new, *([cache_ki] * pc))


def _sample_select_kernel(i_ref, nv_ref, o_ref, thr_ref, *, cw, k_sel):
    o_ref[...] = i_ref[...]
    thr_ref[...] = _select_threshold(o_ref, o_ref.shape[1] // cw, cw, nv_ref[...], k_sel)


def _sample_select(scores, n_valid, *, k_sel):
    r, c = scores.shape
    return pl.pallas_call(
        functools.partial(_sample_select_kernel, cw=LANES * 4 if c % (LANES * 4) == 0 else LANES,
                          k_sel=k_sel),
        out_shape=[jax.ShapeDtypeStruct((r, c), F32), jax.ShapeDtypeStruct((r, 1), F32)],
        compiler_params=pltpu.CompilerParams(vmem_limit_bytes=VMEM_LIMIT_BYTES),
        name="sample_select",
    )(scores, n_valid)


def _sample_attend_kernel(pt_ref, q_ref, thr_ref, ip_ref, in_ref, kn_ref, vn_ref, *rest, pc, t_dec):
    k_pages, v_pages = rest[:pc], rest[pc:2 * pc]
    o_ref, kb_ref, vb_ref, m_ref, l_ref, acc_ref = rest[2 * pc:]
    c = pl.program_id(1)
    nt = (((1,), (1,)), ((), ()))
    page = k_pages[0].shape[1]
    q = q_ref[0]

    @pl.when(c == 0)
    def _():
        m_ref[...] = jnp.full(m_ref.shape, MASKED_SCORE, F32)
        l_ref[...] = jnp.zeros(l_ref.shape, F32)
        acc_ref[...] = jnp.zeros(acc_ref.shape, F32)

    def update(k, v, scores_of):
        s = lax.dot_general(q, k, nt, preferred_element_type=F32)
        parts = []
        for t in range(t_dec):
            keep = scores_of(t) >= thr_ref[0, t:t + 1, :]
            parts.append(jnp.where(keep, s[t * N_HEADS:(t + 1) * N_HEADS], MASKED_SCORE))
        s = jnp.concatenate(parts, axis=0)
        m_old = m_ref[...]
        m_new = jnp.maximum(m_old, jnp.max(s, axis=1, keepdims=True))
        p = jnp.exp(s - m_new).astype(BF16)
        alpha = jnp.exp(m_old - m_new)
        l_ref[...] = alpha * l_ref[...] + jnp.sum(p.astype(F32), axis=1, keepdims=True)
        acc_ref[...] = alpha * acc_ref[...] + jnp.dot(p, v, preferred_element_type=F32)
        m_ref[...] = m_new

    for j in range(pc):
        kb_ref[j * page:(j + 1) * page, :] = k_pages[j][0].astype(BF16)
        vb_ref[j * page:(j + 1) * page, :] = v_pages[j][0].astype(BF16)
    update(kb_ref[...], vb_ref[...], lambda t: ip_ref[0, t:t + 1, :])

    @pl.when(c == pl.num_programs(1) - 1)
    def _():
        update(kn_ref[0], vn_ref[0], lambda t: in_ref[0, t:t + 1, :])
        o_ref[0] = acc_ref[...] / l_ref[...]


def _sample_attend(page_table, q_bd, thr, i_past, i_new, k_new, v_new, cache_k, cache_v, *, pc, t_dec):
    bd, n_pages = page_table.shape
    page, kv = cache_k.shape[1], cache_k.shape[2]
    n_ch = n_pages // pc
    rows = q_bd.shape[1]
    grid_spec = pltpu.PrefetchScalarGridSpec(
        num_scalar_prefetch=1,
        grid=(bd, n_ch),
        in_specs=[pl.BlockSpec((1, rows, kv), lambda b, c, pt: (b, 0, 0)),
                  pl.BlockSpec((1, t_dec, 1), lambda b, c, pt: (b, 0, 0)),
                  pl.BlockSpec((1, t_dec, pc * page), lambda b, c, pt: (b, 0, c)),
                  pl.BlockSpec((1, t_dec, LANES), lambda b, c, pt: (b, 0, 0)),
                  pl.BlockSpec((1, LANES, kv), lambda b, c, pt: (b, 0, 0)),
                  pl.BlockSpec((1, LANES, kv), lambda b, c, pt: (b, 0, 0))]
                 + _page_specs(n_pages, (page, kv), pc) + _page_specs(n_pages, (page, kv), pc),
        out_specs=pl.BlockSpec((1, rows, kv), lambda b, c, pt: (b, 0, 0)),
        scratch_shapes=[pltpu.VMEM((pc * page, kv), BF16), pltpu.VMEM((pc * page, kv), BF16),
                        pltpu.VMEM((rows, 1), F32), pltpu.VMEM((rows, 1), F32),
                        pltpu.VMEM((rows, kv), F32)],
    )
    return pl.pallas_call(
        functools.partial(_sample_attend_kernel, pc=pc, t_dec=t_dec),
        grid_spec=grid_spec,
        out_shape=jax.ShapeDtypeStruct((bd, rows, kv), F32),
        compiler_params=_cparams("parallel", "arbitrary"),
        name="sample_attend",
    )(page_table, q_bd, thr, i_past, i_new, k_new, v_new, *([cache_k] * pc), *([cache_v] * pc))


def _row_tile(r, cap):
    tr = min(r, cap)
    assert r % tr == 0, (r, tr)
    return tr


def _pad_rows(a, n):
    return jnp.pad(a, ((0, 0), (0, n - a.shape[1]), (0, 0)))


def kernel(x_prompt, x_sample, c_prompt, c_sample, state_pool, cache_k, cache_v, cache_kidx, page_table,
           ada_w, ada_b, norm_g, pool_w, pool_scale, attn_w_in, attn_w_o, ffn_w_in, ffn_w_out, final_g):
    b, s, d = x_prompt.shape
    bd, t_dec, _ = x_sample.shape
    depth = ada_w.shape[0]
    assert depth == 2 and d == N_HEADS * HEAD_DIM
    n_pages = page_table.shape[1]
    page = cache_k.shape[2]
    past = n_pages * page
    kv = N_KV_HEADS * HEAD_DIM

    bc = -(-(b + bd) // SUBLANES) * SUBLANES
    c_all = jnp.pad(jnp.concatenate([c_prompt, c_sample], axis=0), ((0, bc - b - bd), (0, 0)))
    mod = _modulation(c_all, ada_w.reshape(depth * 2, d, 3 * d), ada_b.reshape(depth * 2, 1, 3 * d))

    def mods(layer, sub, lo, n, repeat=None):
        m = mod[layer * 2 + sub, lo:lo + n]
        out = []
        for j in range(3):
            mj = m[:, None, j * d:(j + 1) * d]
            if repeat is not None:
                mj = jnp.broadcast_to(mj, (n, repeat, d)).reshape(1, n * repeat, d)
            out.append(mj)
        return out

    g_vec = lambda i, j: norm_g[i, j].reshape(1, d)
    pool_w_bf = pool_w[0].astype(BF16)
    pool_ps = pool_scale[0].reshape(1, d)
    win_bf = ffn_w_in.astype(BF16)
    wout_bf = ffn_w_out.astype(BF16)
    wo_bf = attn_w_o[0].astype(BF16)
    in_cols = attn_w_in.shape[-1]
    w_in_bf = jnp.pad(attn_w_in[0], ((0, 0), (0, -in_cols % LANES))).astype(BF16)
    final_vec = final_g.reshape(1, d)

    tr = _row_tile(s, 512)
    sh, sc, gt = mods(0, 0, 0, b)
    x1, pool_p = _pool_layer(x_prompt, jnp.zeros((b, 16, d), F32), sh, sc, gt, g_vec(0, 0),
                             pool_w_bf, pool_ps, tr=tr, t_valid=tr, pos0=0)
    sh, sc, gt = mods(0, 1, 0, b)
    x2 = _ffn_layer(x1, sh, sc, gt, g_vec(0, 1), win_bf[0], wout_bf[0], tr=tr)
    sh, sc, gta = mods(1, 0, 0, b)
    k_p, v_p, ki_p, wi_p, q_hm, qi_hm, k_hm, v_aug, ki_bf = _dsa_project(
        x2, sh, sc, g_vec(1, 0), w_in_bf, tr=tr)
    att = _dsa_prompt(qi_hm, wi_p, q_hm, ki_bf, k_hm, v_aug,
                      tq=_row_tile(s, 128), tk=_row_tile(s, 512))
    sh, sc, gt = mods(1, 1, 0, b)
    y_prompt = _ffn_layer(x2, sh, sc, gt, g_vec(1, 1), win_bf[1], wout_bf[1], tr=tr,
                          attn=(att, gta, wo_bf), final_g=final_vec)

    rs = bd * t_dec
    t_pad = -(-t_dec // SUBLANES) * SUBLANES
    sh, sc, gt = mods(0, 0, b, bd)
    prefix = jnp.pad(state_pool[0], ((0, 0), (16 - POOL_STATE, 0), (0, 0)))
    x1s, pool_s = _pool_layer(_pad_rows(x_sample, t_pad), prefix, sh, sc, gt, g_vec(0, 0),
                              pool_w_bf, pool_ps, tr=t_pad, t_valid=t_dec, pos0=past)
    x1s = x1s[:, :t_dec].reshape(1, rs, d)
    sh, sc, gt = mods(0, 1, b, bd, repeat=t_dec)
    x2s = _ffn_layer(x1s, sh, sc, gt, g_vec(0, 1), win_bf[0], wout_bf[0], tr=rs)
    sh, sc, gta = mods(1, 0, b, bd, repeat=t_dec)
    k_s, v_s, ki_s, wi_s, q_s, qi_s, _, _, _ = _dsa_project(x2s, sh, sc, g_vec(1, 0), w_in_bf, tr=rs)

    qi_rows = qi_s[0].reshape(N_IDX_HEADS, bd, t_dec, IDX_DIM).transpose(1, 2, 0, 3)
    qi_rows = qi_rows.reshape(bd, t_dec * N_IDX_HEADS, IDX_DIM)
    wi_rows = wi_s.reshape(bd, t_dec * N_IDX_HEADS, 1)
    q_rows = q_s[0].reshape(N_KV_HEADS, GQA_GROUP, bd, t_dec, HEAD_DIM).transpose(2, 3, 0, 1, 4)
    eye = jnp.eye(N_KV_HEADS, dtype=BF16)
    q_bd = (q_rows[:, :, :, :, None, :] * eye[None, None, :, None, :, None])
    q_bd = q_bd.reshape(bd, t_dec * N_HEADS, kv)
    ki_new = _pad_rows(ki_s.reshape(bd, t_dec, IDX_DIM), LANES).astype(BF16)
    k_new = _pad_rows(k_s.reshape(bd, t_dec, kv), LANES).astype(BF16)
    v_new = _pad_rows(v_s.reshape(bd, t_dec, kv), LANES).astype(BF16)

    pc = 16 if n_pages % 16 == 0 else 1
    i_past, i_new = _sample_scores(page_table, qi_rows, wi_rows, ki_new, cache_kidx[0],
                                   pc=pc, t_dec=t_dec)
    scores = jnp.concatenate([i_past, i_new], axis=-1).reshape(rs, past + LANES)
    n_valid = (past + 1 + jnp.arange(rs, dtype=jnp.int32) % t_dec).astype(F32).reshape(rs, 1)
    scores, thr = _sample_select(scores, n_valid, k_sel=min(TOPK_MAX, (past + t_dec) // 4))
    scores = scores.reshape(bd, t_dec, past + LANES)
    att_s = _sample_attend(page_table, q_bd, thr.reshape(bd, t_dec, 1),
                           scores[:, :, :past], scores[:, :, past:], k_new, v_new,
                           cache_k[0].reshape(-1, page, kv), cache_v[0].reshape(-1, page, kv),
                           pc=pc, t_dec=t_dec)
    att_s = att_s.reshape(bd, t_dec, N_KV_HEADS, GQA_GROUP, N_KV_HEADS, HEAD_DIM)
    att_s = jnp.stack([att_s[:, :, n, :, n, :] for n in range(N_KV_HEADS)], axis=2)
    att_s = att_s.reshape(1, rs, N_HEADS * HEAD_DIM).astype(BF16)
    sh, sc, gt = mods(1, 1, b, bd, repeat=t_dec)
    y_sample = _ffn_layer(x2s, sh, sc, gt, g_vec(1, 1), win_bf[1], wout_bf[1], tr=rs,
                          attn=(att_s, gta, wo_bf), final_g=final_vec)

    return (y_prompt,
            y_sample.reshape(bd, t_dec, d),
            pool_p[None],
            k_p.reshape(1, b, s, N_KV_HEADS, HEAD_DIM),
            v_p.reshape(1, b, s, N_KV_HEADS, HEAD_DIM),
            ki_p[None],
            pool_s[None],
            k_s.reshape(1, bd, t_dec, N_KV_HEADS, HEAD_DIM),
            v_s.reshape(1, bd, t_dec, N_KV_HEADS, HEAD_DIM),
            ki_s.reshape(1, bd, t_dec, IDX_DIM))
```

```python
import functools

import jax
import jax.numpy as jnp
from jax import lax
from jax.experimental import pallas as pl
from jax.experimental.pallas import tpu as pltpu

F32 = jnp.float32
BF16 = jnp.bfloat16

N_HEADS = 16
N_KV_HEADS = 4
GQA_GROUP = N_HEADS // N_KV_HEADS
HEAD_DIM = 64
N_IDX_HEADS = 8
IDX_DIM = 64
TOPK_MAX = 256
POOL_WINDOWS = (2, 4, 8, 16)
POOL_STATE = max(POOL_WINDOWS) - 1
RMS_EPS = 1e-6

LANES = 128
SUBLANES = 8
VMEM_LIMIT_BYTES = 56 * 1024 * 1024
MASKED_SCORE = -1e30
DENORMAL_KEYS = 0x007FFFFF
KEY_LOWEST = -2139095040 + DENORMAL_KEYS
KEY_POS_INF = 0x7F800000 - DENORMAL_KEYS
LOG2_E = 1.4426950408889634
SELECT_MAX_ITERS = 34
SELECT_CHUNK = 1024
COUNT_ACC_ROWS = 64
POOL_HALO = 24
NT_DIMS = (((1,), (1,)), ((), ()))


def _cparams(*semantics):
    return pltpu.CompilerParams(dimension_semantics=semantics,
                                vmem_limit_bytes=VMEM_LIMIT_BYTES)


def _silu(x):
    return x * jax.nn.sigmoid(x)


def _normmod(x, g, sc, sh):
    ms = jnp.mean(x * x, axis=-1, keepdims=True)
    return x * lax.rsqrt(ms + RMS_EPS) * g * (1.0 + sc) + sh


def _resident(shape):
    nd = len(shape)
    return pl.BlockSpec(shape, lambda *_: (0,) * nd)


def _mod_kernel(c_ref, w_ref, b_ref, o_ref):
    a = _silu(c_ref[...]).astype(BF16)
    o_ref[0] = jnp.dot(a, w_ref[0].astype(BF16), preferred_element_type=F32) + b_ref[0]


def _modulation(c_all, ada_w, ada_b):
    n_l, d, d3 = ada_w.shape
    bc = c_all.shape[0]
    tn = d3 // 2
    return pl.pallas_call(
        _mod_kernel,
        grid=(n_l, d3 // tn),
        in_specs=[pl.BlockSpec((bc, d), lambda l, j: (0, 0)),
                  pl.BlockSpec((1, d, tn), lambda l, j: (l, 0, j)),
                  pl.BlockSpec((1, 1, tn), lambda l, j: (l, 0, j))],
        out_specs=pl.BlockSpec((1, bc, tn), lambda l, j: (l, 0, j)),
        out_shape=jax.ShapeDtypeStruct((n_l, bc, d3), F32),
        compiler_params=_cparams("parallel", "parallel"),
        name="modulation",
    )(c_all, ada_w, ada_b)


def _pool_kernel(x_ref, pre_ref, sh_ref, sc_ref, gt_ref, g_ref, w_ref, ps_ref,
                 o_ref, st_ref, e_ref, b1_ref, b2_ref, b3_ref, *, tr, t_valid, pos0):
    t = pl.program_id(1)
    d = x_ref.shape[-1]
    grp = d // len(POOL_WINDOWS)
    top = tr + POOL_HALO

    @pl.when(t == 0)
    def _():
        zeros = jnp.zeros((SUBLANES, d), F32)
        e_ref[0:SUBLANES, :] = zeros
        b1_ref[0:SUBLANES, :] = zeros
        b2_ref[0:SUBLANES, :] = zeros
        e_ref[SUBLANES:POOL_HALO, :] = pre_ref[0]

    x = x_ref[0]
    h = _normmod(x, g_ref[...], sc_ref[0], sh_ref[0])
    e_ref[POOL_HALO:top, :] = h
    b1_ref[SUBLANES:top, :] = e_ref[SUBLANES:top, :] + e_ref[SUBLANES - 1:top - 1, :]
    b2_ref[SUBLANES:top, grp:] = b1_ref[SUBLANES:top, grp:] + b1_ref[SUBLANES - 2:top - 2, grp:]
    b3_ref[SUBLANES:top, 2 * grp:] = (b2_ref[SUBLANES:top, 2 * grp:]
                                       + b2_ref[SUBLANES - 4:top - 4, 2 * grp:])
    sums = (b1_ref[POOL_HALO:top, 0:grp],
            b2_ref[POOL_HALO:top, grp:2 * grp],
            b3_ref[POOL_HALO:top, 2 * grp:3 * grp],
            b3_ref[POOL_HALO:top, 3 * grp:] + b3_ref[POOL_HALO - 8:top - 8, 3 * grp:])

    pos = pos0 + t * tr + lax.broadcasted_iota(jnp.int32, (tr, 1), 0)
    ys = []
    for gi, w in enumerate(POOL_WINDOWS):
        inv_cnt = 1.0 / jnp.minimum(w, pos + 1).astype(F32)
        diff = sums[gi] * inv_cnt - h[:, gi * grp:(gi + 1) * grp]
        ys.append(jnp.dot(diff.astype(BF16), w_ref[gi], preferred_element_type=F32))
    y = jnp.concatenate(ys, axis=-1) * ps_ref[...]
    o_ref[0] = x + gt_ref[0] * y

    st_ref[0] = e_ref[POOL_HALO + t_valid - POOL_STATE:POOL_HALO + t_valid, :]
    e_ref[SUBLANES:POOL_HALO, :] = e_ref[tr + SUBLANES:top, :]


def _pool_layer(x, prefix, sh, sc, gt, g, w_bf, ps, *, tr, t_valid, pos0):
    b, r, d = x.shape
    grp = d // len(POOL_WINDOWS)
    nt = r // tr
    row = pl.BlockSpec((1, tr, d), lambda i, j: (i, j, 0))
    per_b = pl.BlockSpec((1, 1, d), lambda i, j: (i, 0, 0))
    vec = pl.BlockSpec((1, d), lambda i, j: (0, 0))
    rows = tr + POOL_HALO
    return pl.pallas_call(
        functools.partial(_pool_kernel, tr=tr, t_valid=t_valid, pos0=pos0),
        grid=(b, nt),
        in_specs=[row,
                  pl.BlockSpec((1, 16, d), lambda i, j: (i, 0, 0)),
                  per_b, per_b, per_b, vec,
                  _resident((len(POOL_WINDOWS), grp, grp)),
                  vec],
        out_specs=[row, pl.BlockSpec((1, POOL_STATE, d), lambda i, j: (i, 0, 0))],
        out_shape=[jax.ShapeDtypeStruct((b, r, d), F32),
                   jax.ShapeDtypeStruct((b, POOL_STATE, d), F32)],
        scratch_shapes=[pltpu.VMEM((rows, d), F32)] * 4,
        compiler_params=_cparams("parallel", "arbitrary"),
        name="pool_layer",
    )(x, prefix, sh, sc, gt, g, w_bf, ps)


def _ffn_kernel(*refs, has_attn, has_final, n_chunks):
    it = iter(refs)
    x_ref = next(it)
    if has_attn:
        att_ref, gta_ref, wo_ref = next(it), next(it), next(it)
    sh_ref, sc_ref, gt_ref, g_ref, win_ref, wout_ref = (next(it) for _ in range(6))
    if has_final:
        fg_ref = next(it)
    o_ref = next(it)

    x = x_ref[0]
    if has_attn:
        x = x + gta_ref[0] * jnp.dot(att_ref[0], wo_ref[...], preferred_element_type=F32)
    h = _normmod(x, g_ref[...], sc_ref[0], sh_ref[0]).astype(BF16)
    f = wout_ref.shape[0]
    fc = f // n_chunks
    acc = jnp.zeros(x.shape, F32)
    for c in range(n_chunks):
        gate = jnp.dot(h, win_ref[:, c * fc:(c + 1) * fc], preferred_element_type=F32)
        up = jnp.dot(h, win_ref[:, f + c * fc:f + (c + 1) * fc], preferred_element_type=F32)
        act = (_silu(gate) * up).astype(BF16)
        acc = acc + jnp.dot(act, wout_ref[c * fc:(c + 1) * fc, :], preferred_element_type=F32)
    y = x + gt_ref[0] * acc
    if has_final:
        ms = jnp.mean(y * y, axis=-1, keepdims=True)
        y = y * lax.rsqrt(ms + RMS_EPS) * fg_ref[...]
    o_ref[0] = y


def _mod_spec(m, tr):
    d = m.shape[-1]
    if m.shape[1] == 1:
        return pl.BlockSpec((1, 1, d), lambda i, j: (i, 0, 0))
    return pl.BlockSpec((1, tr, d), lambda i, j: (i, j, 0))


def _ffn_layer(x, sh, sc, gt, g, win_bf, wout_bf, *, tr, attn=None, final_g=None):
    b, r, d = x.shape
    f = wout_bf.shape[0]
    row = pl.BlockSpec((1, tr, d), lambda i, j: (i, j, 0))
    vec = pl.BlockSpec((1, d), lambda i, j: (0, 0))
    args, specs = [x], [row]
    if attn is not None:
        att, gta, wo_bf = attn
        args += [att, gta, wo_bf]
        specs += [pl.BlockSpec((1, tr, att.shape[-1]), lambda i, j: (i, j, 0)),
                  _mod_spec(gta, tr), _resident(wo_bf.shape)]
    args += [sh, sc, gt, g, win_bf, wout_bf]
    specs += [_mod_spec(sh, tr), _mod_spec(sc, tr), _mod_spec(gt, tr), vec,
              _resident(win_bf.shape), _resident(wout_bf.shape)]
    if final_g is not None:
        args.append(final_g)
        specs.append(vec)
    n_chunks = 2 if f % (2 * LANES) == 0 else 1
    return pl.pallas_call(
        functools.partial(_ffn_kernel, has_attn=attn is not None,
                          has_final=final_g is not None, n_chunks=n_chunks),
        grid=(b, r // tr),
        in_specs=specs,
        out_specs=row,
        out_shape=jax.ShapeDtypeStruct((b, r, d), F32),
        compiler_params=_cparams("parallel", "parallel"),
        name="ffn_layer",
    )(*args)


def _proj_kernel(x_ref, sh_ref, sc_ref, g_ref, w_ref,
                 k_ref, v_ref, ki_ref, wit_ref, q_ref, qi_ref, kh_ref, vt_ref, kib_ref):
    tr = x_ref.shape[1]
    kv = N_KV_HEADS * HEAD_DIM
    o_k = N_HEADS * HEAD_DIM
    o_v = o_k + kv
    o_qi = o_v + kv
    o_ki = o_qi + N_IDX_HEADS * IDX_DIM
    h = _normmod(x_ref[0], g_ref[...], sc_ref[0], sh_ref[0]).astype(BF16)
    p = jnp.dot(h, w_ref[...], preferred_element_type=F32)
    k_ref[0] = p[:, o_k:o_v]
    v_ref[0] = p[:, o_v:o_qi]
    ki = p[:, o_ki:o_ki + IDX_DIM]
    ki_ref[0] = ki
    kib_ref[0] = ki.astype(BF16)
    tail_t = p[:, o_ki:o_ki + LANES].T
    wit_ref[0] = tail_t[IDX_DIM:IDX_DIM + N_IDX_HEADS] * (N_IDX_HEADS ** -0.5 * IDX_DIM ** -0.5)
    for hd in range(N_HEADS):
        q_ref[0, hd] = (p[:, hd * HEAD_DIM:(hd + 1) * HEAD_DIM]
                        * (HEAD_DIM ** -0.5 * LOG2_E)).astype(BF16)
    for hd in range(N_IDX_HEADS):
        qi_ref[0, hd] = p[:, o_qi + hd * IDX_DIM:o_qi + (hd + 1) * IDX_DIM].astype(BF16)
    ones_pad = (lax.broadcasted_iota(jnp.int32, (tr, LANES - HEAD_DIM), 1) == 0).astype(F32)
    for n in range(N_KV_HEADS):
        kh_ref[0, n] = p[:, o_k + n * HEAD_DIM:o_k + (n + 1) * HEAD_DIM].astype(BF16)
        vn = p[:, o_v + n * HEAD_DIM:o_v + (n + 1) * HEAD_DIM]
        vt_ref[0, n] = jnp.concatenate([vn, ones_pad], axis=-1).T.astype(BF16)


def _dsa_project(x, sh, sc, g, w_bf, *, tr):
    b, r, d = x.shape
    kv = N_KV_HEADS * HEAD_DIM
    row = lambda n: pl.BlockSpec((1, tr, n), lambda i, j: (i, j, 0))
    heads = lambda nh, n: pl.BlockSpec((1, nh, tr, n), lambda i, j: (i, 0, j, 0))
    sds = jax.ShapeDtypeStruct
    return pl.pallas_call(
        _proj_kernel,
        grid=(b, r // tr),
        in_specs=[row(d), _mod_spec(sh, tr), _mod_spec(sc, tr),
                  pl.BlockSpec((1, d), lambda i, j: (0, 0)), _resident(w_bf.shape)],
        out_specs=[row(kv), row(kv), row(IDX_DIM),
                   pl.BlockSpec((1, N_IDX_HEADS, tr), lambda i, j: (i, 0, j)),
                   heads(N_HEADS, HEAD_DIM), heads(N_IDX_HEADS, IDX_DIM),
                   heads(N_KV_HEADS, HEAD_DIM),
                   pl.BlockSpec((1, N_KV_HEADS, LANES, tr), lambda i, j: (i, 0, 0, j)),
                   row(IDX_DIM)],
        out_shape=[sds((b, r, kv), F32), sds((b, r, kv), F32), sds((b, r, IDX_DIM), F32),
                   sds((b, N_IDX_HEADS, r), F32),
                   sds((b, N_HEADS, r, HEAD_DIM), BF16), sds((b, N_IDX_HEADS, r, IDX_DIM), BF16),
                   sds((b, N_KV_HEADS, r, HEAD_DIM), BF16), sds((b, N_KV_HEADS, LANES, r), BF16),
                   sds((b, r, IDX_DIM), BF16)],
        compiler_params=_cparams("parallel", "parallel"),
        name="dsa_project",
    )(x, sh, sc, g, w_bf)


def _key_to_f32(k):
    k = jnp.where(k > 0, k + DENORMAL_KEYS, jnp.where(k < -1, k - DENORMAL_KEYS, k))
    return lax.bitcast_convert_type(jnp.where(k >= 0, k, k ^ 0x7FFFFFFF), F32)


def _count(s_ref, n_chunks, rc, pred):
    q = s_ref.shape[1]
    ar = min(rc, COUNT_ACC_ROWS)

    def body(c, acc):
        r0 = pl.multiple_of(c * rc, rc)
        hit = jnp.where(pred(s_ref[pl.ds(r0, rc), :], r0), 1.0, 0.0)
        return acc + jnp.sum(hit.reshape(rc // ar, ar, q), axis=0)

    acc = lax.fori_loop(0, n_chunks, body, jnp.zeros((ar, q), F32))
    return jnp.sum(acc, axis=0, keepdims=True)


def _select_threshold(s_ref, n_chunks, rc, n_valid, k_sel):
    q = s_ref.shape[1]
    kf = float(k_sel)

    def midpoint(lo, hi):
        return jnp.where(jnp.logical_and(lo == 0, hi > 1), 1, (lo & hi) + ((lo ^ hi) >> 1))

    def cond(st):
        return jnp.logical_and(st[0] < SELECT_MAX_ITERS, st[1] > 0.0)

    def body(st):
        it, _, lo, hi, c_lo, c_hi = st
        mid = midpoint(lo, hi)
        active = jnp.logical_and(c_lo > kf, mid != lo)
        thr = _key_to_f32(mid)
        c = _count(s_ref, n_chunks, rc, lambda x, r0: x >= thr)
        ge = c >= kf
        up = jnp.logical_and(active, ge)
        dn = jnp.logical_and(active, jnp.logical_not(ge))
        lo = jnp.where(up, mid, lo)
        c_lo = jnp.where(up, c, c_lo)
        hi = jnp.where(dn, mid, hi)
        c_hi = jnp.where(dn, c, c_hi)
        still = jnp.logical_and(c_lo > kf, midpoint(lo, hi) != lo)
        return it + 1, jnp.sum(jnp.where(still, 1.0, 0.0)), lo, hi, c_lo, c_hi

    st = lax.while_loop(cond, body,
                        (jnp.int32(0), jnp.sum(jnp.where(n_valid > kf, 1.0, 0.0)),
                         jnp.full((1, q), KEY_LOWEST, jnp.int32), jnp.full((1, q), KEY_POS_INF, jnp.int32),
                         n_valid, jnp.zeros((1, q), F32)))
    _, _, lo, _, c_lo, c_hi = st
    thr = _key_to_f32(lo)
    tied = c_lo > kf
    need = kf - c_hi

    @pl.when(jnp.sum(jnp.where(tied, 1.0, 0.0)) > 0.0)
    def _():
        def key_ids(r0, shape):
            return r0 + lax.broadcasted_iota(jnp.int32, shape, 0)

        def jbody(_, st):
            jlo, jhi = st
            jm = (jlo + jhi) >> 1
            c = _count(s_ref, n_chunks, rc,
                       lambda x, r0: jnp.logical_and(x == thr, key_ids(r0, x.shape) <= jm))
            ok = c >= need
            return jnp.where(ok, jlo, jm), jnp.where(ok, jm, jhi)

        n_it = max(1, (s_ref.shape[0] - 1).bit_length())
        _, jcut = lax.fori_loop(0, n_it, jbody,
                                (jnp.full((1, q), -1, jnp.int32),
                                 jnp.zeros((1, q), jnp.int32) + (n_chunks * rc - 1)))

        def drop(c, carry):
            r0 = pl.multiple_of(c * rc, rc)
            x = s_ref[pl.ds(r0, rc), :]
            kill = jnp.logical_and(jnp.logical_and(tied, x == thr), key_ids(r0, x.shape) > jcut)
            s_ref[pl.ds(r0, rc), :] = jnp.where(kill, -jnp.inf, x)
            return carry

        lax.fori_loop(0, n_chunks, drop, 0)

    return thr


def _dsa_prompt_kernel(qi_ref, wit_ref, q_ref, ki_ref, k_ref, vt_ref, o_ref,
                       s_ref, m_ref, acc_ref, sa_ref, sb_ref, pa_ref, pb_ref,
                       *, tq, tk, ts, fill, k_sel):
    s_len = s_ref.shape[0]
    q0 = pl.program_id(1) * tq
    n_fill = (q0 + tq + fill - 1) // fill
    q_ids = q0 + lax.broadcasted_iota(jnp.int32, (1, tq), 1)
    gq = GQA_GROUP * tq

    def scores(kt, carry):
        r0 = pl.multiple_of(kt * ts, ts)
        ki = ki_ref[0, pl.ds(r0, ts), :]
        acc = jnp.zeros((ts, tq), F32)
        for j in range(N_IDX_HEADS // 2):
            qi2 = qi_ref[0, 2 * j:2 * j + 2].reshape(2 * tq, IDX_DIM)
            s2 = lax.dot_general(ki, qi2, NT_DIMS, preferred_element_type=F32)
            for u in range(2):
                hd = 2 * j + u
                acc = acc + jnp.maximum(s2[:, u * tq:(u + 1) * tq], 0.0) * wit_ref[0, hd:hd + 1, :]
        k_ids = r0 + lax.broadcasted_iota(jnp.int32, (ts, 1), 0)
        s_ref[pl.ds(r0, ts), :] = jnp.where(k_ids <= q_ids, acc, -jnp.inf)
        return carry

    lax.fori_loop(0, n_fill * (fill // ts), scores, 0)

    thr = _select_threshold(s_ref, n_fill, fill, (q_ids + 1).astype(F32), k_sel)

    def raw_scores(kt, dst_ref):
        r0 = pl.multiple_of(jnp.minimum(kt * tk, s_len - tk), tk)
        for n in range(N_KV_HEADS):
            qn = q_ref[0, n * GQA_GROUP:(n + 1) * GQA_GROUP].reshape(gq, HEAD_DIM)
            dst_ref[n] = lax.dot_general(k_ref[0, n, pl.ds(r0, tk), :], qn, NT_DIMS,
                                         preferred_element_type=F32)

    def pv(kt, p_ref, n):
        r0 = pl.multiple_of(jnp.maximum(kt, 0) * tk, tk)
        return jnp.dot(vt_ref[0, n, :, pl.ds(r0, tk)], p_ref[n], preferred_element_type=F32)

    def step(kt, s_cur, s_nxt, p_cur, p_prv):
        raw_scores(kt + 1, s_nxt)
        r0 = pl.multiple_of(kt * tk, tk)
        bias = jnp.where(s_ref[pl.ds(r0, tk), :] >= thr, 0.0, MASKED_SCORE)
        bias = jnp.concatenate([bias] * GQA_GROUP, axis=1)
        for n in range(N_KV_HEADS):
            s = s_cur[n] + bias
            m_old = m_ref[n]
            m_new = jnp.maximum(m_old, jnp.max(s, axis=0, keepdims=True))
            p_cur[n] = jnp.exp2(s - m_new).astype(BF16)
            acc_ref[n] = jnp.exp2(m_old - m_new) * (acc_ref[n] + pv(kt - 1, p_prv, n))
            m_ref[n] = m_new

    m_ref[...] = jnp.full(m_ref.shape, MASKED_SCORE, F32)
    acc_ref[...] = jnp.zeros(acc_ref.shape, F32)
    pb_ref[...] = jnp.zeros(pb_ref.shape, BF16)
    raw_scores(0, sa_ref)
    n_pairs = (q0 + tq + 2 * tk - 1) // (2 * tk)

    def pair(i, carry):
        step(2 * i, sa_ref, sb_ref, pa_ref, pb_ref)
        step(2 * i + 1, sb_ref, sa_ref, pb_ref, pa_ref)
        return carry

    lax.fori_loop(0, n_pairs, pair, 0)

    for n in range(N_KV_HEADS):
        a = (acc_ref[n] + pv(2 * n_pairs - 1, pb_ref, n)).T
        o = a[:, 0:HEAD_DIM] / a[:, HEAD_DIM:HEAD_DIM + 1]
        for g in range(GQA_GROUP):
            hd = n * GQA_GROUP + g
            o_ref[0, :, hd * HEAD_DIM:(hd + 1) * HEAD_DIM] = o[g * tq:(g + 1) * tq].astype(o_ref.dtype)


def _dsa_prompt(qi_hm, wit, q_hm, ki_bf, k_hm, vt):
    b, _, s, _ = q_hm.shape
    k_sel = min(TOPK_MAX, s // 4)
    tq = _row_tile(s, 128)
    tk = _row_tile(s, 256)
    ts = _row_tile(s, 512)
    fill = _row_tile(s, SELECT_CHUNK)
    assert fill % (2 * tk) == 0 and fill % ts == 0
    gq = GQA_GROUP * tq
    return pl.pallas_call(
        functools.partial(_dsa_prompt_kernel, tq=tq, tk=tk, ts=ts, fill=fill, k_sel=k_sel),
        grid=(b, s // tq),
        in_specs=[pl.BlockSpec((1, N_IDX_HEADS, tq, IDX_DIM), lambda i, j: (i, 0, j, 0)),
                  pl.BlockSpec((1, N_IDX_HEADS, tq), lambda i, j: (i, 0, j)),
                  pl.BlockSpec((1, N_HEADS, tq, HEAD_DIM), lambda i, j: (i, 0, j, 0)),
                  pl.BlockSpec((1, s, IDX_DIM), lambda i, j: (i, 0, 0)),
                  pl.BlockSpec((1, N_KV_HEADS, s, HEAD_DIM), lambda i, j: (i, 0, 0, 0)),
                  pl.BlockSpec((1, N_KV_HEADS, LANES, s), lambda i, j: (i, 0, 0, 0))],
        out_specs=pl.BlockSpec((1, tq, N_HEADS * HEAD_DIM), lambda i, j: (i, j, 0)),
        out_shape=jax.ShapeDtypeStruct((b, s, N_HEADS * HEAD_DIM), BF16),
        scratch_shapes=[pltpu.VMEM((s, tq), F32),
                        pltpu.VMEM((N_KV_HEADS, 1, gq), F32),
                        pltpu.VMEM((N_KV_HEADS, LANES, gq), F32),
                        pltpu.VMEM((N_KV_HEADS, tk, gq), F32),
                        pltpu.VMEM((N_KV_HEADS, tk, gq), F32),
                        pltpu.VMEM((N_KV_HEADS, tk, gq), BF16),
                        pltpu.VMEM((N_KV_HEADS, tk, gq), BF16)],
        compiler_params=_cparams("parallel", "arbitrary"),
        name="dsa_prompt",
    )(qi_hm, wit, q_hm, ki_bf, k_hm, vt)


def _page_specs(page_shape, pc):
    zeros = (0,) * len(page_shape)

    def spec(j):
        return pl.BlockSpec((1,) + page_shape, lambda b, c, pt: (pt[b, c * pc + j],) + zeros)
    return [spec(j) for j in range(pc)]


def _sample_scores_kernel(pt_ref, qi_ref, wi_ref, kin_ref, *rest, pc, t_dec):
    pages, (ip_ref, in_ref) = rest[:pc], rest[pc:]
    c = pl.program_id(1)
    qi = qi_ref[0]
    wi = wi_ref[0]
    page = pages[0].shape[2]

    def score(ki_t):
        s = jnp.dot(qi, ki_t, preferred_element_type=F32)
        s = jnp.maximum(s, 0.0) * wi
        return jnp.sum(s.reshape(t_dec, N_IDX_HEADS, s.shape[-1]), axis=1)

    for j in range(pc):
        ip_ref[0, :, j * page:(j + 1) * page] = score(pages[j][0].astype(BF16))

    @pl.when(c == pl.num_programs(1) - 1)
    def _():
        s = score(kin_ref[0])
        col = lax.broadcasted_iota(jnp.int32, s.shape, 1)
        row = lax.broadcasted_iota(jnp.int32, s.shape, 0)
        in_ref[0] = jnp.where(col <= row, s, -jnp.inf)


def _sample_scores(page_table, qi_s, wi_s, ki_new_t, cache_ki_t, *, pc, t_dec):
    bd, n_pages = page_table.shape
    page = cache_ki_t.shape[2]
    rows = qi_s.shape[1]
    grid_spec = pltpu.PrefetchScalarGridSpec(
        num_scalar_prefetch=1,
        grid=(bd, n_pages // pc),
        in_specs=[pl.BlockSpec((1, rows, IDX_DIM), lambda b, c, pt: (b, 0, 0)),
                  pl.BlockSpec((1, rows, 1), lambda b, c, pt: (b, 0, 0)),
                  pl.BlockSpec((1, IDX_DIM, LANES), lambda b, c, pt: (b, 0, 0))]
                 + _page_specs((IDX_DIM, page), pc),
        out_specs=[pl.BlockSpec((1, t_dec, pc * page), lambda b, c, pt: (b, 0, c)),
                   pl.BlockSpec((1, t_dec, LANES), lambda b, c, pt: (b, 0, 0))],
    )
    return pl.pallas_call(
        functools.partial(_sample_scores_kernel, pc=pc, t_dec=t_dec),
        grid_spec=grid_spec,
        out_shape=[jax.ShapeDtypeStruct((bd, t_dec, n_pages * page), F32),
                   jax.ShapeDtypeStruct((bd, t_dec, LANES), F32)],
        compiler_params=_cparams("parallel", "arbitrary"),
        name="sample_scores",
    )(page_table, qi_s, wi_s, ki_new_t, *([cache_ki_t] * pc))


def _sample_select_kernel(s_ref, nv_ref, o_ref, thr_ref, *, k_sel):
    o_ref[...] = s_ref[...]
    thr_ref[...] = _select_threshold(o_ref, o_ref.shape[0] // SELECT_CHUNK, SELECT_CHUNK,
                                     nv_ref[...], k_sel)


def _sample_select(scores_t, n_valid, *, k_sel):
    r, q = scores_t.shape
    return pl.pallas_call(
        functools.partial(_sample_select_kernel, k_sel=k_sel),
        out_shape=[jax.ShapeDtypeStruct((r, q), F32), jax.ShapeDtypeStruct((1, q), F32)],
        compiler_params=pltpu.CompilerParams(vmem_limit_bytes=VMEM_LIMIT_BYTES),
        name="sample_select",
    )(scores_t, n_valid)


def _sample_attend_kernel(pt_ref, q_ref, thr_ref, ip_ref, in_ref, kn_ref, vn_ref, *rest, pc, t_dec):
    k_pages, v_pages = rest[:pc], rest[pc:2 * pc]
    o_ref, kb_ref, vb_ref, m_ref, l_ref, acc_ref = rest[2 * pc:]
    c = pl.program_id(1)
    page = k_pages[0].shape[3]

    @pl.when(c == 0)
    def _():
        m_ref[...] = jnp.full(m_ref.shape, MASKED_SCORE, F32)
        l_ref[...] = jnp.zeros(l_ref.shape, F32)
        acc_ref[...] = jnp.zeros(acc_ref.shape, F32)

    def update(k_t, v_t, scores):
        bias = jnp.where(scores >= thr_ref[0], 0.0, MASKED_SCORE)
        bias = jnp.concatenate([bias] * GQA_GROUP, axis=0)
        for n in range(N_KV_HEADS):
            s = jnp.dot(q_ref[0, n], k_t(n), preferred_element_type=F32) + bias
            m_old = m_ref[n]
            m_new = jnp.maximum(m_old, jnp.max(s, axis=1, keepdims=True))
            p = jnp.exp2(s - m_new).astype(BF16)
            alpha = jnp.exp2(m_old - m_new)
            l_ref[n] = alpha * l_ref[n] + jnp.sum(p.astype(F32), axis=1, keepdims=True)
            acc_ref[n] = alpha * acc_ref[n] + lax.dot_general(p, v_t(n), NT_DIMS,
                                                              preferred_element_type=F32)
            m_ref[n] = m_new

    for j in range(pc):
        kb_ref[:, :, j * page:(j + 1) * page] = k_pages[j][0].astype(BF16)
        vb_ref[:, :, j * page:(j + 1) * page] = v_pages[j][0].astype(BF16)
    update(lambda n: kb_ref[n], lambda n: vb_ref[n], ip_ref[0])

    @pl.when(c == pl.num_programs(1) - 1)
    def _():
        update(lambda n: kn_ref[0, n], lambda n: vn_ref[0, n], in_ref[0])
        for n in range(N_KV_HEADS):
            o_ref[0, n] = acc_ref[n] / l_ref[n]


def _sample_attend(page_table, q_s, thr, i_past, i_new, k_new_t, v_new_t, cache_k_t, cache_v_t,
                   *, pc, t_dec):
    bd, n_pages = page_table.shape
    page = cache_k_t.shape[3]
    rows = q_s.shape[2]
    per_b4 = lambda shape: pl.BlockSpec((1,) + shape, lambda b, c, pt: (b, 0, 0, 0))
    per_b3 = lambda shape: pl.BlockSpec((1,) + shape, lambda b, c, pt: (b, 0, 0))
    grid_spec = pltpu.PrefetchScalarGridSpec(
        num_scalar_prefetch=1,
        grid=(bd, n_pages // pc),
        in_specs=[per_b4((N_KV_HEADS, rows, HEAD_DIM)),
                  per_b3((t_dec, 1)),
                  pl.BlockSpec((1, t_dec, pc * page), lambda b, c, pt: (b, 0, c)),
                  per_b3((t_dec, LANES)),
                  per_b4((N_KV_HEADS, HEAD_DIM, LANES)),
                  per_b4((N_KV_HEADS, HEAD_DIM, LANES))]
                 + _page_specs((N_KV_HEADS, HEAD_DIM, page), pc) * 2,
        out_specs=per_b4((N_KV_HEADS, rows, HEAD_DIM)),
        scratch_shapes=[pltpu.VMEM((N_KV_HEADS, HEAD_DIM, pc * page), BF16),
                        pltpu.VMEM((N_KV_HEADS, HEAD_DIM, pc * page), BF16),
                        pltpu.VMEM((N_KV_HEADS, rows, 1), F32),
                        pltpu.VMEM((N_KV_HEADS, rows, 1), F32),
                        pltpu.VMEM((N_KV_HEADS, rows, HEAD_DIM), F32)],
    )
    return pl.pallas_call(
        functools.partial(_sample_attend_kernel, pc=pc, t_dec=t_dec),
        grid_spec=grid_spec,
        out_shape=jax.ShapeDtypeStruct((bd, N_KV_HEADS, rows, HEAD_DIM), F32),
        compiler_params=_cparams("parallel", "arbitrary"),
        name="sample_attend",
    )(page_table, q_s, thr, i_past, i_new, k_new_t, v_new_t,
      *([cache_k_t] * pc), *([cache_v_t] * pc))


def _row_tile(r, cap):
    tr = min(r, cap)
    assert r % tr == 0, (r, tr)
    return tr


def _pad_axis(a, axis, n, value=0.0):
    pad = [(0, 0)] * a.ndim
    pad[axis] = (0, n - a.shape[axis])
    return jnp.pad(a, pad, constant_values=value)


def kernel(x_prompt, x_sample, c_prompt, c_sample, state_pool, cache_k, cache_v, cache_kidx, page_table,
           ada_w, ada_b, norm_g, pool_w, pool_scale, attn_w_in, attn_w_o, ffn_w_in, ffn_w_out, final_g):
    b, s, d = x_prompt.shape
    bd, t_dec, _ = x_sample.shape
    depth = ada_w.shape[0]
    assert depth == 2 and d == N_HEADS * HEAD_DIM
    n_pages = page_table.shape[1]
    page = cache_k.shape[2]
    past = n_pages * page
    kv = N_KV_HEADS * HEAD_DIM

    bc = -(-(b + bd) // SUBLANES) * SUBLANES
    c_all = _pad_axis(jnp.concatenate([c_prompt, c_sample], axis=0), 0, bc)
    mod = _modulation(c_all, ada_w.reshape(depth * 2, d, 3 * d), ada_b.reshape(depth * 2, 1, 3 * d))

    def mods(layer, sub, lo, n, repeat=None):
        m = mod[layer * 2 + sub, lo:lo + n]
        out = []
        for j in range(3):
            mj = m[:, None, j * d:(j + 1) * d]
            if repeat is not None:
                mj = jnp.broadcast_to(mj, (n, repeat, d)).reshape(1, n * repeat, d)
            out.append(mj)
        return out

    g_vec = lambda i, j: norm_g[i, j].reshape(1, d)
    pool_w_bf = pool_w[0].astype(BF16)
    pool_ps = pool_scale[0].reshape(1, d)
    win_bf = ffn_w_in.astype(BF16)
    wout_bf = ffn_w_out.astype(BF16)
    wo_bf = attn_w_o[0].astype(BF16)
    in_cols = attn_w_in.shape[-1]
    w_in_bf = _pad_axis(attn_w_in[0], 1, in_cols + (-in_cols % LANES)).astype(BF16)
    final_vec = final_g.reshape(1, d)

    tr = _row_tile(s, 512)
    sh, sc, gt = mods(0, 0, 0, b)
    x1, pool_p = _pool_layer(x_prompt, jnp.zeros((b, 16, d), F32), sh, sc, gt, g_vec(0, 0),
                             pool_w_bf, pool_ps, tr=tr, t_valid=tr, pos0=0)
    sh, sc, gt = mods(0, 1, 0, b)
    x2 = _ffn_layer(x1, sh, sc, gt, g_vec(0, 1), win_bf[0], wout_bf[0], tr=tr)
    sh, sc, gta = mods(1, 0, 0, b)
    k_p, v_p, ki_p, wit_p, q_hm, qi_hm, k_hm, vt_p, ki_bf = _dsa_project(
        x2, sh, sc, g_vec(1, 0), w_in_bf, tr=tr)
    att = _dsa_prompt(qi_hm, wit_p, q_hm, ki_bf, k_hm, vt_p)
    sh, sc, gt = mods(1, 1, 0, b)
    y_prompt = _ffn_layer(x2, sh, sc, gt, g_vec(1, 1), win_bf[1], wout_bf[1], tr=tr,
                          attn=(att, gta, wo_bf), final_g=final_vec)

    rs = bd * t_dec
    t_pad = -(-t_dec // SUBLANES) * SUBLANES
    sh, sc, gt = mods(0, 0, b, bd)
    prefix = jnp.pad(state_pool[0], ((0, 0), (16 - POOL_STATE, 0), (0, 0)))
    x1s, pool_s = _pool_layer(_pad_axis(x_sample, 1, t_pad), prefix, sh, sc, gt, g_vec(0, 0),
                              pool_w_bf, pool_ps, tr=t_pad, t_valid=t_dec, pos0=past)
    x1s = x1s[:, :t_dec].reshape(1, rs, d)
    sh, sc, gt = mods(0, 1, b, bd, repeat=t_dec)
    x2s = _ffn_layer(x1s, sh, sc, gt, g_vec(0, 1), win_bf[0], wout_bf[0], tr=rs)
    sh, sc, gta = mods(1, 0, b, bd, repeat=t_dec)
    k_s, v_s, ki_s, wit_s, q_s, qi_s, _, _, _ = _dsa_project(x2s, sh, sc, g_vec(1, 0), w_in_bf, tr=rs)

    qi_rows = qi_s[0].reshape(N_IDX_HEADS, bd, t_dec, IDX_DIM).transpose(1, 2, 0, 3)
    qi_rows = qi_rows.reshape(bd, t_dec * N_IDX_HEADS, IDX_DIM)
    wi_rows = wit_s[0].reshape(N_IDX_HEADS, bd, t_dec).transpose(1, 2, 0)
    wi_rows = wi_rows.reshape(bd, t_dec * N_IDX_HEADS, 1)
    q_rows = q_s[0].reshape(N_KV_HEADS, GQA_GROUP, bd, t_dec, HEAD_DIM).transpose(2, 0, 1, 3, 4)
    q_rows = q_rows.reshape(bd, N_KV_HEADS, GQA_GROUP * t_dec, HEAD_DIM)
    ki_new_t = _pad_axis(ki_s.reshape(bd, t_dec, IDX_DIM).transpose(0, 2, 1), 2, LANES).astype(BF16)
    new_t = lambda a: _pad_axis(a.reshape(bd, t_dec, N_KV_HEADS, HEAD_DIM).transpose(0, 2, 3, 1),
                                3, LANES).astype(BF16)
    cache_k_t = cache_k[0].transpose(0, 2, 3, 1)
    cache_v_t = cache_v[0].transpose(0, 2, 3, 1)
    cache_ki_t = cache_kidx[0].transpose(0, 2, 1)

    pc = 16 if n_pages % 16 == 0 else 1
    i_past, i_new = _sample_scores(page_table, qi_rows, wi_rows, ki_new_t, cache_ki_t,
                                   pc=pc, t_dec=t_dec)
    n_keys = past + LANES
    n_keys_pad = -(-n_keys // SELECT_CHUNK) * SELECT_CHUNK
    scores_t = jnp.concatenate([i_past, i_new], axis=-1).reshape(rs, n_keys).T
    scores_t = _pad_axis(scores_t, 0, n_keys_pad, -jnp.inf)
    n_valid = (past + 1 + jnp.arange(rs, dtype=jnp.int32) % t_dec).astype(F32).reshape(1, rs)
    scores_t, thr = _sample_select(scores_t, n_valid, k_sel=min(TOPK_MAX, (past + t_dec) // 4))
    scores = scores_t[:n_keys].T.reshape(bd, t_dec, n_keys)
    att_s = _sample_attend(page_table, q_rows, thr.reshape(bd, t_dec, 1),
                           scores[:, :, :past], scores[:, :, past:], new_t(k_s), new_t(v_s),
                           cache_k_t, cache_v_t, pc=pc, t_dec=t_dec)
    att_s = att_s.reshape(bd, N_KV_HEADS, GQA_GROUP, t_dec, HEAD_DIM).transpose(0, 3, 1, 2, 4)
    att_s = att_s.reshape(1, rs, N_HEADS * HEAD_DIM).astype(BF16)
    sh, sc, gt = mods(1, 1, b, bd, repeat=t_dec)
    y_sample = _ffn_layer(x2s, sh, sc, gt, g_vec(1, 1), win_bf[1], wout_bf[1], tr=rs,
                          attn=(att_s, gta, wo_bf), final_g=final_vec)

    return (y_prompt,
            y_sample.reshape(bd, t_dec, d),
            pool_p[None],
            k_p.reshape(1, b, s, N_KV_HEADS, HEAD_DIM),
            v_p.reshape(1, b, s, N_KV_HEADS, HEAD_DIM),
            ki_p[None],
            pool_s[None],
            k_s.reshape(1, bd, t_dec, N_KV_HEADS, HEAD_DIM),
            v_s.reshape(1, bd, t_dec, N_KV_HEADS, HEAD_DIM),
            ki_s.reshape(1, bd, t_dec, IDX_DIM))
```

```python
import functools

import jax
import jax.numpy as jnp
from jax import lax
from jax.experimental import pallas as pl
from jax.experimental.pallas import tpu as pltpu

F32 = jnp.float32
BF16 = jnp.bfloat16

N_HEADS = 16
N_KV_HEADS = 4
GQA_GROUP = N_HEADS // N_KV_HEADS
HEAD_DIM = 64
N_IDX_HEADS = 8
IDX_DIM = 64
TOPK_MAX = 256
POOL_WINDOWS = (2, 4, 8, 16)
POOL_STATE = max(POOL_WINDOWS) - 1
RMS_EPS = 1e-6

LANES = 128
SUBLANES = 8
VMEM_LIMIT_BYTES = 56 * 1024 * 1024
MASKED_SCORE = -1e30
SOFTMAX_SUM_FLOOR = 2.0 ** -100
DENORMAL_KEYS = 0x007FFFFF
KEY_LOWEST = -2139095040 + DENORMAL_KEYS
KEY_POS_INF = 0x7F800000 - DENORMAL_KEYS
LOG2_E = 1.4426950408889634
SELECT_MAX_ITERS = 40
PROBE_SPAN_KEYS = 4 << 23
SELECT_CHUNK = 1024
COUNT_ACC_ROWS = 64
POOL_HALO = 24
VT_ROWS = HEAD_DIM + 16
NT_DIMS = (((1,), (1,)), ((), ()))


def _cparams(*semantics):
    return pltpu.CompilerParams(dimension_semantics=semantics,
                                vmem_limit_bytes=VMEM_LIMIT_BYTES)


def _silu(x):
    return x * jax.nn.sigmoid(x)


def _normmod(x, g, sc, sh):
    ms = jnp.mean(x * x, axis=-1, keepdims=True)
    return x * lax.rsqrt(ms + RMS_EPS) * g * (1.0 + sc) + sh


def _resident(shape):
    nd = len(shape)
    return pl.BlockSpec(shape, lambda *_: (0,) * nd)


def _mod_kernel(c_ref, w_ref, b_ref, o_ref):
    a = _silu(c_ref[...]).astype(BF16)
    o_ref[0] = jnp.dot(a, w_ref[0].astype(BF16), preferred_element_type=F32) + b_ref[0]


def _modulation(c_all, ada_w, ada_b):
    n_l, d, d3 = ada_w.shape
    bc = c_all.shape[0]
    tn = d3 // 2
    return pl.pallas_call(
        _mod_kernel,
        grid=(n_l, d3 // tn),
        in_specs=[pl.BlockSpec((bc, d), lambda l, j: (0, 0)),
                  pl.BlockSpec((1, d, tn), lambda l, j: (l, 0, j)),
                  pl.BlockSpec((1, 1, tn), lambda l, j: (l, 0, j))],
        out_specs=pl.BlockSpec((1, bc, tn), lambda l, j: (l, 0, j)),
        out_shape=jax.ShapeDtypeStruct((n_l, bc, d3), F32),
        compiler_params=_cparams("parallel", "parallel"),
        name="modulation",
    )(c_all, ada_w, ada_b)


def _pool_kernel(x_ref, pre_ref, sh_ref, sc_ref, gt_ref, g_ref, w_ref, ps_ref,
                 o_ref, st_ref, e_ref, b1_ref, b2_ref, b3_ref, *, tr, t_valid, pos0):
    t = pl.program_id(1)
    d = x_ref.shape[-1]
    grp = d // len(POOL_WINDOWS)
    top = tr + POOL_HALO

    @pl.when(t == 0)
    def _():
        zeros = jnp.zeros((SUBLANES, d), F32)
        e_ref[0:SUBLANES, :] = zeros
        b1_ref[0:SUBLANES, :] = zeros
        b2_ref[0:SUBLANES, :] = zeros
        e_ref[SUBLANES:POOL_HALO, :] = pre_ref[0]

    x = x_ref[0]
    h = _normmod(x, g_ref[...], sc_ref[0], sh_ref[0])
    e_ref[POOL_HALO:top, :] = h
    b1_ref[SUBLANES:top, :] = e_ref[SUBLANES:top, :] + e_ref[SUBLANES - 1:top - 1, :]
    b2_ref[SUBLANES:top, grp:] = b1_ref[SUBLANES:top, grp:] + b1_ref[SUBLANES - 2:top - 2, grp:]
    b3_ref[SUBLANES:top, 2 * grp:] = (b2_ref[SUBLANES:top, 2 * grp:]
                                       + b2_ref[SUBLANES - 4:top - 4, 2 * grp:])
    sums = (b1_ref[POOL_HALO:top, 0:grp],
            b2_ref[POOL_HALO:top, grp:2 * grp],
            b3_ref[POOL_HALO:top, 2 * grp:3 * grp],
            b3_ref[POOL_HALO:top, 3 * grp:] + b3_ref[POOL_HALO - 8:top - 8, 3 * grp:])

    pos = pos0 + t * tr + lax.broadcasted_iota(jnp.int32, (tr, 1), 0)
    ys = []
    for gi, w in enumerate(POOL_WINDOWS):
        inv_cnt = 1.0 / jnp.minimum(w, pos + 1).astype(F32)
        diff = sums[gi] * inv_cnt - h[:, gi * grp:(gi + 1) * grp]
        ys.append(jnp.dot(diff.astype(BF16), w_ref[gi], preferred_element_type=F32))
    y = jnp.concatenate(ys, axis=-1) * ps_ref[...]
    o_ref[0] = x + gt_ref[0] * y

    st_ref[0] = e_ref[POOL_HALO + t_valid - POOL_STATE:POOL_HALO + t_valid, :]
    e_ref[SUBLANES:POOL_HALO, :] = e_ref[tr + SUBLANES:top, :]


def _pool_layer(x, prefix, sh, sc, gt, g, w_bf, ps, *, tr, t_valid, pos0):
    b, r, d = x.shape
    grp = d // len(POOL_WINDOWS)
    nt = r // tr
    row = pl.BlockSpec((1, tr, d), lambda i, j: (i, j, 0))
    per_b = pl.BlockSpec((1, 1, d), lambda i, j: (i, 0, 0))
    vec = pl.BlockSpec((1, d), lambda i, j: (0, 0))
    rows = tr + POOL_HALO
    return pl.pallas_call(
        functools.partial(_pool_kernel, tr=tr, t_valid=t_valid, pos0=pos0),
        grid=(b, nt),
        in_specs=[row,
                  pl.BlockSpec((1, 16, d), lambda i, j: (i, 0, 0)),
                  per_b, per_b, per_b, vec,
                  _resident((len(POOL_WINDOWS), grp, grp)),
                  vec],
        out_specs=[row, pl.BlockSpec((1, POOL_STATE, d), lambda i, j: (i, 0, 0))],
        out_shape=[jax.ShapeDtypeStruct((b, r, d), F32),
                   jax.ShapeDtypeStruct((b, POOL_STATE, d), F32)],
        scratch_shapes=[pltpu.VMEM((rows, d), F32)] * 4,
        compiler_params=_cparams("parallel", "arbitrary"),
        name="pool_layer",
    )(x, prefix, sh, sc, gt, g, w_bf, ps)


def _ffn_kernel(*refs, has_attn, has_final, n_chunks):
    it = iter(refs)
    x_ref = next(it)
    if has_attn:
        att_ref, gta_ref, wo_ref = next(it), next(it), next(it)
    sh_ref, sc_ref, gt_ref, g_ref, win_ref, wout_ref = (next(it) for _ in range(6))
    if has_final:
        fg_ref = next(it)
    o_ref = next(it)

    x = x_ref[0]
    if has_attn:
        x = x + gta_ref[0] * jnp.dot(att_ref[0], wo_ref[...], preferred_element_type=F32)
    h = _normmod(x, g_ref[...], sc_ref[0], sh_ref[0]).astype(BF16)
    f = wout_ref.shape[0]
    fc = f // n_chunks
    acc = jnp.zeros(x.shape, F32)
    for c in range(n_chunks):
        gate = jnp.dot(h, win_ref[:, c * fc:(c + 1) * fc], preferred_element_type=F32)
        up = jnp.dot(h, win_ref[:, f + c * fc:f + (c + 1) * fc], preferred_element_type=F32)
        act = (_silu(gate) * up).astype(BF16)
        acc = acc + jnp.dot(act, wout_ref[c * fc:(c + 1) * fc, :], preferred_element_type=F32)
    y = x + gt_ref[0] * acc
    if has_final:
        ms = jnp.mean(y * y, axis=-1, keepdims=True)
        y = y * lax.rsqrt(ms + RMS_EPS) * fg_ref[...]
    o_ref[0] = y


def _mod_spec(m, tr):
    d = m.shape[-1]
    if m.shape[1] == 1:
        return pl.BlockSpec((1, 1, d), lambda i, j: (i, 0, 0))
    return pl.BlockSpec((1, tr, d), lambda i, j: (i, j, 0))


def _ffn_layer(x, sh, sc, gt, g, win_bf, wout_bf, *, tr, attn=None, final_g=None):
    b, r, d = x.shape
    f = wout_bf.shape[0]
    row = pl.BlockSpec((1, tr, d), lambda i, j: (i, j, 0))
    vec = pl.BlockSpec((1, d), lambda i, j: (0, 0))
    args, specs = [x], [row]
    if attn is not None:
        att, gta, wo_bf = attn
        args += [att, gta, wo_bf]
        specs += [pl.BlockSpec((1, tr, att.shape[-1]), lambda i, j: (i, j, 0)),
                  _mod_spec(gta, tr), _resident(wo_bf.shape)]
    args += [sh, sc, gt, g, win_bf, wout_bf]
    specs += [_mod_spec(sh, tr), _mod_spec(sc, tr), _mod_spec(gt, tr), vec,
              _resident(win_bf.shape), _resident(wout_bf.shape)]
    if final_g is not None:
        args.append(final_g)
        specs.append(vec)
    n_chunks = 2 if f % (2 * LANES) == 0 else 1
    return pl.pallas_call(
        functools.partial(_ffn_kernel, has_attn=attn is not None,
                          has_final=final_g is not None, n_chunks=n_chunks),
        grid=(b, r // tr),
        in_specs=specs,
        out_specs=row,
        out_shape=jax.ShapeDtypeStruct((b, r, d), F32),
        compiler_params=_cparams("parallel", "parallel"),
        name="ffn_layer",
    )(*args)


def _proj_kernel(x_ref, sh_ref, sc_ref, g_ref, w_ref,
                 k_ref, v_ref, ki_ref, wit_ref, q_ref, qi_ref, kh_ref, vt_ref, kib_ref):
    tr = x_ref.shape[1]
    kv = N_KV_HEADS * HEAD_DIM
    o_k = N_HEADS * HEAD_DIM
    o_v = o_k + kv
    o_qi = o_v + kv
    o_ki = o_qi + N_IDX_HEADS * IDX_DIM
    h = _normmod(x_ref[0], g_ref[...], sc_ref[0], sh_ref[0]).astype(BF16)
    p = jnp.dot(h, w_ref[...], preferred_element_type=F32)
    k_ref[0] = p[:, o_k:o_v]
    v_ref[0] = p[:, o_v:o_qi]
    ki = p[:, o_ki:o_ki + IDX_DIM]
    ki_ref[0] = ki
    kib_ref[0] = ki.astype(BF16)
    tail_t = p[:, o_ki:o_ki + LANES].T
    wit_ref[0] = tail_t[IDX_DIM:IDX_DIM + N_IDX_HEADS] * (N_IDX_HEADS ** -0.5 * IDX_DIM ** -0.5)
    for hd in range(N_HEADS):
        q_ref[0, hd] = (p[:, hd * HEAD_DIM:(hd + 1) * HEAD_DIM]
                        * (HEAD_DIM ** -0.5 * LOG2_E)).astype(BF16)
    for hd in range(N_IDX_HEADS):
        qi_ref[0, hd] = p[:, o_qi + hd * IDX_DIM:o_qi + (hd + 1) * IDX_DIM].astype(BF16)
    ones_pad = (lax.broadcasted_iota(jnp.int32, (tr, LANES - HEAD_DIM), 1) == 0).astype(F32)
    for n in range(N_KV_HEADS):
        kh_ref[0, n] = p[:, o_k + n * HEAD_DIM:o_k + (n + 1) * HEAD_DIM].astype(BF16)
        vn = p[:, o_v + n * HEAD_DIM:o_v + (n + 1) * HEAD_DIM]
        vt_ref[0, n] = jnp.concatenate([vn, ones_pad], axis=-1).T[0:VT_ROWS].astype(BF16)


def _dsa_project(x, sh, sc, g, w_bf, *, tr):
    b, r, d = x.shape
    kv = N_KV_HEADS * HEAD_DIM
    row = lambda n: pl.BlockSpec((1, tr, n), lambda i, j: (i, j, 0))
    heads = lambda nh, n: pl.BlockSpec((1, nh, tr, n), lambda i, j: (i, 0, j, 0))
    sds = jax.ShapeDtypeStruct
    return pl.pallas_call(
        _proj_kernel,
        grid=(b, r // tr),
        in_specs=[row(d), _mod_spec(sh, tr), _mod_spec(sc, tr),
                  pl.BlockSpec((1, d), lambda i, j: (0, 0)), _resident(w_bf.shape)],
        out_specs=[row(kv), row(kv), row(IDX_DIM),
                   pl.BlockSpec((1, N_IDX_HEADS, tr), lambda i, j: (i, 0, j)),
                   heads(N_HEADS, HEAD_DIM), heads(N_IDX_HEADS, IDX_DIM),
                   heads(N_KV_HEADS, HEAD_DIM),
                   pl.BlockSpec((1, N_KV_HEADS, VT_ROWS, tr), lambda i, j: (i, 0, 0, j)),
                   row(IDX_DIM)],
        out_shape=[sds((b, r, kv), F32), sds((b, r, kv), F32), sds((b, r, IDX_DIM), F32),
                   sds((b, N_IDX_HEADS, r), F32),
                   sds((b, N_HEADS, r, HEAD_DIM), BF16), sds((b, N_IDX_HEADS, r, IDX_DIM), BF16),
                   sds((b, N_KV_HEADS, r, HEAD_DIM), BF16), sds((b, N_KV_HEADS, VT_ROWS, r), BF16),
                   sds((b, r, IDX_DIM), BF16)],
        compiler_params=_cparams("parallel", "parallel"),
        name="dsa_project",
    )(x, sh, sc, g, w_bf)


def _key_to_f32(k):
    k = jnp.where(k > 0, k + DENORMAL_KEYS, jnp.where(k < -1, k - DENORMAL_KEYS, k))
    return lax.bitcast_convert_type(jnp.where(k >= 0, k, k ^ 0x7FFFFFFF), F32)


def _f32_to_key(v):
    b = lax.bitcast_convert_type(v, jnp.int32)
    k = jnp.where(b >= 0, b, b ^ 0x7FFFFFFF)
    return jnp.where(k > DENORMAL_KEYS, k - DENORMAL_KEYS,
                     jnp.where(k < -1 - DENORMAL_KEYS, k + DENORMAL_KEYS,
                               jnp.where(k >= 0, 0, -1)))


def _count(s_ref, n_chunks, rc, pred):
    q = s_ref.shape[1]
    ar = min(rc, COUNT_ACC_ROWS)

    def body(c, acc):
        r0 = pl.multiple_of(c * rc, rc)
        hit = jnp.where(pred(s_ref[pl.ds(r0, rc), :], r0), 1.0, 0.0)
        return acc + jnp.sum(hit.reshape(rc // ar, ar, q), axis=0)

    acc = lax.fori_loop(0, n_chunks, body, jnp.zeros((ar, q), F32))
    return jnp.sum(acc, axis=0, keepdims=True)


def _select_threshold(s_ref, n_chunks, rc, n_valid, k_sel, top=None):
    q = s_ref.shape[1]
    kf = float(k_sel)
    if top is not None:
        top_key = _f32_to_key(top)

    def midpoint(lo, hi):
        return jnp.where(jnp.logical_and(lo == 0, hi > 1), 1, (lo & hi) + ((lo ^ hi) >> 1))

    def cond(st):
        return jnp.logical_and(st[0] < SELECT_MAX_ITERS, st[1] > 0.0)

    def body(st):
        it, _, lo, hi, c_lo, c_hi = st
        mid = midpoint(lo, hi)
        active = jnp.logical_and(c_lo > kf, mid != lo)
        if top is not None:
            probe = jnp.where(it == 0, top_key, top_key - PROBE_SPAN_KEYS)
            use = jnp.logical_and(it < 2, jnp.logical_and(probe > lo, probe < hi))
            mid = jnp.where(use, probe, mid)
        thr = _key_to_f32(mid)
        c = _count(s_ref, n_chunks, rc, lambda x, r0: x >= thr)
        ge = c >= kf
        up = jnp.logical_and(active, ge)
        dn = jnp.logical_and(active, jnp.logical_not(ge))
        lo = jnp.where(up, mid, lo)
        c_lo = jnp.where(up, c, c_lo)
        hi = jnp.where(dn, mid, hi)
        c_hi = jnp.where(dn, c, c_hi)
        still = jnp.logical_and(c_lo > kf, midpoint(lo, hi) != lo)
        return it + 1, jnp.sum(jnp.where(still, 1.0, 0.0)), lo, hi, c_lo, c_hi

    st = lax.while_loop(cond, body,
                        (jnp.int32(0), jnp.sum(jnp.where(n_valid > kf, 1.0, 0.0)),
                         jnp.full((1, q), KEY_LOWEST, jnp.int32), jnp.full((1, q), KEY_POS_INF, jnp.int32),
                         n_valid, jnp.zeros((1, q), F32)))
    _, _, lo, _, c_lo, c_hi = st
    thr = _key_to_f32(lo)
    tied = c_lo > kf
    need = kf - c_hi

    @pl.when(jnp.sum(jnp.where(tied, 1.0, 0.0)) > 0.0)
    def _():
        def key_ids(r0, shape):
            return r0 + lax.broadcasted_iota(jnp.int32, shape, 0)

        def jbody(_, st):
            jlo, jhi = st
            jm = (jlo + jhi) >> 1
            c = _count(s_ref, n_chunks, rc,
                       lambda x, r0: jnp.logical_and(x == thr, key_ids(r0, x.shape) <= jm))
            ok = c >= need
            return jnp.where(ok, jlo, jm), jnp.where(ok, jm, jhi)

        n_it = max(1, (s_ref.shape[0] - 1).bit_length())
        _, jcut = lax.fori_loop(0, n_it, jbody,
                                (jnp.full((1, q), -1, jnp.int32),
                                 jnp.zeros((1, q), jnp.int32) + (n_chunks * rc - 1)))

        def drop(c, carry):
            r0 = pl.multiple_of(c * rc, rc)
            x = s_ref[pl.ds(r0, rc), :]
            kill = jnp.logical_and(jnp.logical_and(tied, x == thr), key_ids(r0, x.shape) > jcut)
            s_ref[pl.ds(r0, rc), :] = jnp.where(kill, -jnp.inf, x)
            return carry

        lax.fori_loop(0, n_chunks, drop, 0)

    return thr


def _dsa_prompt_kernel(qi_ref, wit_ref, q_ref, ki_ref, k_ref, vt_ref, o_ref,
                       s_ref, m_ref, acc_ref, sa_ref, sb_ref, pa_ref, pb_ref,
                       *, tq, tk, ts, fill, k_sel):
    s_len = s_ref.shape[0]
    q0 = pl.program_id(1) * tq
    n_fill = (q0 + tq + fill - 1) // fill
    q_ids = q0 + lax.broadcasted_iota(jnp.int32, (1, tq), 1)
    gq = GQA_GROUP * tq

    def scores(kt, top):
        r0 = pl.multiple_of(kt * ts, ts)
        ki = ki_ref[0, pl.ds(r0, ts), :]
        acc = jnp.zeros((ts, tq), F32)
        for j in range(N_IDX_HEADS // 2):
            qi2 = qi_ref[0, 2 * j:2 * j + 2].reshape(2 * tq, IDX_DIM)
            s2 = lax.dot_general(ki, qi2, NT_DIMS, preferred_element_type=F32)
            for u in range(2):
                hd = 2 * j + u
                acc = acc + jnp.maximum(s2[:, u * tq:(u + 1) * tq], 0.0) * wit_ref[0, hd:hd + 1, :]
        k_ids = r0 + lax.broadcasted_iota(jnp.int32, (ts, 1), 0)
        visible = jnp.where(k_ids <= q_ids, acc, -jnp.inf)
        s_ref[pl.ds(r0, ts), :] = visible
        return jnp.maximum(top, jnp.max(visible, axis=0, keepdims=True))

    top = lax.fori_loop(0, n_fill * (fill // ts), scores, jnp.full((1, tq), -jnp.inf, F32))

    thr = _select_threshold(s_ref, n_fill, fill, (q_ids + 1).astype(F32), k_sel, top=top)

    def raw_scores(kt, dst_ref):
        r0 = pl.multiple_of(jnp.minimum(kt * tk, s_len - tk), tk)
        for n in range(N_KV_HEADS):
            qn = q_ref[0, n * GQA_GROUP:(n + 1) * GQA_GROUP].reshape(gq, HEAD_DIM)
            dst_ref[n] = lax.dot_general(k_ref[0, n, pl.ds(r0, tk), :], qn, NT_DIMS,
                                         preferred_element_type=F32)

    def pv(kt, p_ref, n):
        r0 = pl.multiple_of(jnp.maximum(kt, 0) * tk, tk)
        return jnp.dot(vt_ref[0, n, :, pl.ds(r0, tk)], p_ref[n], preferred_element_type=F32)

    def step(kt, s_cur, s_nxt, p_cur, p_prv, masked_max):
        raw_scores(kt + 1, s_nxt)
        r0 = pl.multiple_of(kt * tk, tk)
        keep = s_ref[pl.ds(r0, tk), :] >= thr
        if masked_max:
            bias = jnp.where(keep, 0.0, MASKED_SCORE)
        else:
            keep01 = jnp.where(keep, 1.0, 0.0).astype(BF16)
        for n in range(N_KV_HEADS):
            pv_n = pv(kt - 1, p_prv, n)
            for g in range(GQA_GROUP):
                cols = slice(g * tq, (g + 1) * tq)
                s = s_cur[n, :, cols]
                if masked_max:
                    s = s + bias
                m_old = m_ref[n, :, cols]
                m_new = jnp.maximum(m_old, jnp.max(s, axis=0, keepdims=True))
                p = jnp.exp2(s - m_new).astype(BF16)
                p_cur[n, :, cols] = p if masked_max else p * keep01
                acc_ref[n, :, cols] = jnp.exp2(m_old - m_new) * (acc_ref[n, :, cols] + pv_n[:, cols])
                m_ref[n, :, cols] = m_new

    n_pairs = (q0 + tq + 2 * tk - 1) // (2 * tk)

    def attention(masked_max):
        m_ref[...] = jnp.full(m_ref.shape, MASKED_SCORE, F32)
        acc_ref[...] = jnp.zeros(acc_ref.shape, F32)
        pb_ref[...] = jnp.zeros(pb_ref.shape, BF16)
        raw_scores(0, sa_ref)

        def pair(i, carry):
            step(2 * i, sa_ref, sb_ref, pa_ref, pb_ref, masked_max)
            step(2 * i + 1, sb_ref, sa_ref, pb_ref, pa_ref, masked_max)
            return carry

        lax.fori_loop(0, n_pairs, pair, 0)
        for n in range(N_KV_HEADS):
            acc_ref[n] = acc_ref[n] + pv(2 * n_pairs - 1, pb_ref, n)

    attention(masked_max=False)
    sums = acc_ref[:, HEAD_DIM:HEAD_DIM + 1, :]
    underflowed = jnp.sum(jnp.where(sums >= SOFTMAX_SUM_FLOOR, 0.0, 1.0)) > 0.0

    @pl.when(underflowed)
    def _():
        attention(masked_max=True)

    for n in range(N_KV_HEADS):
        a = acc_ref[n]
        a = jnp.concatenate([a, jnp.zeros((LANES - VT_ROWS, gq), F32)], axis=0).T
        o = a[:, 0:HEAD_DIM] / a[:, HEAD_DIM:HEAD_DIM + 1]
        for g in range(GQA_GROUP):
            hd = n * GQA_GROUP + g
            o_ref[0, :, hd * HEAD_DIM:(hd + 1) * HEAD_DIM] = o[g * tq:(g + 1) * tq].astype(o_ref.dtype)


def _dsa_prompt(qi_hm, wit, q_hm, ki_bf, k_hm, vt):
    b, _, s, _ = q_hm.shape
    k_sel = min(TOPK_MAX, s // 4)
    tq = _row_tile(s, 128)
    tk = _row_tile(s, 256)
    ts = _row_tile(s, 512)
    fill = _row_tile(s, SELECT_CHUNK)
    assert fill % (2 * tk) == 0 and fill % ts == 0
    gq = GQA_GROUP * tq
    return pl.pallas_call(
        functools.partial(_dsa_prompt_kernel, tq=tq, tk=tk, ts=ts, fill=fill, k_sel=k_sel),
        grid=(b, s // tq),
        in_specs=[pl.BlockSpec((1, N_IDX_HEADS, tq, IDX_DIM), lambda i, j: (i, 0, j, 0)),
                  pl.BlockSpec((1, N_IDX_HEADS, tq), lambda i, j: (i, 0, j)),
                  pl.BlockSpec((1, N_HEADS, tq, HEAD_DIM), lambda i, j: (i, 0, j, 0)),
                  pl.BlockSpec((1, s, IDX_DIM), lambda i, j: (i, 0, 0)),
                  pl.BlockSpec((1, N_KV_HEADS, s, HEAD_DIM), lambda i, j: (i, 0, 0, 0)),
                  pl.BlockSpec((1, N_KV_HEADS, VT_ROWS, s), lambda i, j: (i, 0, 0, 0))],
        out_specs=pl.BlockSpec((1, tq, N_HEADS * HEAD_DIM), lambda i, j: (i, j, 0)),
        out_shape=jax.ShapeDtypeStruct((b, s, N_HEADS * HEAD_DIM), BF16),
        scratch_shapes=[pltpu.VMEM((s, tq), F32),
                        pltpu.VMEM((N_KV_HEADS, 1, gq), F32),
                        pltpu.VMEM((N_KV_HEADS, VT_ROWS, gq), F32),
                        pltpu.VMEM((N_KV_HEADS, tk, gq), F32),
                        pltpu.VMEM((N_KV_HEADS, tk, gq), F32),
                        pltpu.VMEM((N_KV_HEADS, tk, gq), BF16),
                        pltpu.VMEM((N_KV_HEADS, tk, gq), BF16)],
        compiler_params=_cparams("parallel", "arbitrary"),
        name="dsa_prompt",
    )(qi_hm, wit, q_hm, ki_bf, k_hm, vt)


def _page_specs(page_shape, pc):
    zeros = (0,) * len(page_shape)

    def spec(j):
        return pl.BlockSpec((1,) + page_shape, lambda b, c, pt: (pt[b, c * pc + j],) + zeros)
    return [spec(j) for j in range(pc)]


def _sample_scores_kernel(pt_ref, qi_ref, wi_ref, kin_ref, *rest, pc, t_dec):
    pages, (ip_ref, in_ref, kb_ref) = rest[:pc], rest[pc:]
    c = pl.program_id(1)
    qi = qi_ref[0]
    wi = wi_ref[0]
    page = pages[0].shape[2]

    def score(ki_t):
        s = jnp.dot(qi, ki_t, preferred_element_type=F32)
        s = jnp.maximum(s, 0.0) * wi
        return jnp.sum(s.reshape(t_dec, N_IDX_HEADS, s.shape[-1]), axis=1)

    for j in range(pc):
        kb_ref[:, j * page:(j + 1) * page] = pages[j][0].astype(BF16)
    ip_ref[0] = score(kb_ref[...])

    @pl.when(c == pl.num_programs(1) - 1)
    def _():
        s = score(kin_ref[0])
        col = lax.broadcasted_iota(jnp.int32, s.shape, 1)
        row = lax.broadcasted_iota(jnp.int32, s.shape, 0)
        in_ref[0] = jnp.where(col <= row, s, -jnp.inf)


def _sample_scores(page_table, qi_s, wi_s, ki_new_t, cache_ki_t, *, pc, t_dec):
    bd, n_pages = page_table.shape
    page = cache_ki_t.shape[2]
    rows = qi_s.shape[1]
    grid_spec = pltpu.PrefetchScalarGridSpec(
        num_scalar_prefetch=1,
        grid=(bd, n_pages // pc),
        in_specs=[pl.BlockSpec((1, rows, IDX_DIM), lambda b, c, pt: (b, 0, 0)),
                  pl.BlockSpec((1, rows, 1), lambda b, c, pt: (b, 0, 0)),
                  pl.BlockSpec((1, IDX_DIM, LANES), lambda b, c, pt: (b, 0, 0))]
                 + _page_specs((IDX_DIM, page), pc),
        out_specs=[pl.BlockSpec((1, t_dec, pc * page), lambda b, c, pt: (b, 0, c)),
                   pl.BlockSpec((1, t_dec, LANES), lambda b, c, pt: (b, 0, 0))],
        scratch_shapes=[pltpu.VMEM((IDX_DIM, pc * page), BF16)],
    )
    return pl.pallas_call(
        functools.partial(_sample_scores_kernel, pc=pc, t_dec=t_dec),
        grid_spec=grid_spec,
        out_shape=[jax.ShapeDtypeStruct((bd, t_dec, n_pages * page), F32),
                   jax.ShapeDtypeStruct((bd, t_dec, LANES), F32)],
        compiler_params=_cparams("parallel", "arbitrary"),
        name="sample_scores",
    )(page_table, qi_s, wi_s, ki_new_t, *([cache_ki_t] * pc))


def _sample_select_kernel(s_ref, nv_ref, o_ref, thr_ref, *, k_sel):
    o_ref[...] = s_ref[...]
    thr_ref[...] = _select_threshold(o_ref, o_ref.shape[0] // SELECT_CHUNK, SELECT_CHUNK,
                                     nv_ref[...], k_sel)


def _sample_select(scores_t, n_valid, *, k_sel):
    r, q = scores_t.shape
    return pl.pallas_call(
        functools.partial(_sample_select_kernel, k_sel=k_sel),
        out_shape=[jax.ShapeDtypeStruct((r, q), F32), jax.ShapeDtypeStruct((1, q), F32)],
        compiler_params=pltpu.CompilerParams(vmem_limit_bytes=VMEM_LIMIT_BYTES),
        name="sample_select",
    )(scores_t, n_valid)


def _sample_attend_kernel(pt_ref, q_ref, thr_ref, ip_ref, in_ref, kn_ref, vn_ref, *rest, pc, t_dec):
    k_pages, v_pages = rest[:pc], rest[pc:2 * pc]
    o_ref, kb_ref, vb_ref, m_ref, l_ref, acc_ref = rest[2 * pc:]
    c = pl.program_id(1)
    page = k_pages[0].shape[3]
    rows, kv = q_ref.shape[1], q_ref.shape[2]

    @pl.when(c == 0)
    def _():
        m_ref[...] = jnp.full(m_ref.shape, MASKED_SCORE, F32)
        l_ref[...] = jnp.zeros(l_ref.shape, F32)
        acc_ref[...] = jnp.zeros(acc_ref.shape, F32)

    def update(k_t, v_t, scores):
        bias = jnp.where(scores >= thr_ref[0], 0.0, MASKED_SCORE)
        if SUBLANES % t_dec == 0:
            bias = jnp.concatenate([bias] * (SUBLANES // t_dec), axis=0)
        bias = jnp.concatenate([bias] * (rows // bias.shape[0]), axis=0)
        s = jnp.dot(q_ref[0], k_t, preferred_element_type=F32) + bias
        m_old = m_ref[...]
        m_new = jnp.maximum(m_old, jnp.max(s, axis=1, keepdims=True))
        p = jnp.exp2(s - m_new).astype(BF16)
        alpha = jnp.exp2(m_old - m_new)
        l_ref[...] = alpha * l_ref[...] + jnp.sum(p.astype(F32), axis=1, keepdims=True)
        acc_ref[...] = alpha * acc_ref[...] + lax.dot_general(p, v_t, NT_DIMS,
                                                              preferred_element_type=F32)
        m_ref[...] = m_new

    for j in range(pc):
        kb_ref[:, j * page:(j + 1) * page] = k_pages[j][0].reshape(kv, page).astype(BF16)
        vb_ref[:, j * page:(j + 1) * page] = v_pages[j][0].reshape(kv, page).astype(BF16)
    update(kb_ref[...], vb_ref[...], ip_ref[0])

    @pl.when(c == pl.num_programs(1) - 1)
    def _():
        update(kn_ref[0], vn_ref[0], in_ref[0])
        o_ref[0] = acc_ref[...] / l_ref[...]


def _sample_attend(page_table, q_s, thr, i_past, i_new, k_new_t, v_new_t, cache_k_t, cache_v_t,
                   *, pc, t_dec):
    bd, n_pages = page_table.shape
    page = cache_k_t.shape[3]
    rows, kv = q_s.shape[1], q_s.shape[2]
    per_b = lambda shape: pl.BlockSpec((1,) + shape, lambda b, c, pt: (b, 0, 0))
    grid_spec = pltpu.PrefetchScalarGridSpec(
        num_scalar_prefetch=1,
        grid=(bd, n_pages // pc),
        in_specs=[per_b((rows, kv)),
                  per_b((t_dec, 1)),
                  pl.BlockSpec((1, t_dec, pc * page), lambda b, c, pt: (b, 0, c)),
                  per_b((t_dec, LANES)),
                  per_b((kv, LANES)),
                  per_b((kv, LANES))]
                 + _page_specs((N_KV_HEADS, HEAD_DIM, page), pc) * 2,
        out_specs=per_b((rows, kv)),
        scratch_shapes=[pltpu.VMEM((kv, pc * page), BF16),
                        pltpu.VMEM((kv, pc * page), BF16),
                        pltpu.VMEM((rows, 1), F32),
                        pltpu.VMEM((rows, 1), F32),
                        pltpu.VMEM((rows, kv), F32)],
    )
    return pl.pallas_call(
        functools.partial(_sample_attend_kernel, pc=pc, t_dec=t_dec),
        grid_spec=grid_spec,
        out_shape=jax.ShapeDtypeStruct((bd, rows, kv), F32),
        compiler_params=_cparams("parallel", "arbitrary"),
        name="sample_attend",
    )(page_table, q_s, thr, i_past, i_new, k_new_t, v_new_t,
      *([cache_k_t] * pc), *([cache_v_t] * pc))


def _row_tile(r, cap):
    tr = min(r, cap)
    assert r % tr == 0, (r, tr)
    return tr


def _pad_axis(a, axis, n, value=0.0):
    pad = [(0, 0)] * a.ndim
    pad[axis] = (0, n - a.shape[axis])
    return jnp.pad(a, pad, constant_values=value)


def kernel(x_prompt, x_sample, c_prompt, c_sample, state_pool, cache_k, cache_v, cache_kidx, page_table,
           ada_w, ada_b, norm_g, pool_w, pool_scale, attn_w_in, attn_w_o, ffn_w_in, ffn_w_out, final_g):
    b, s, d = x_prompt.shape
    bd, t_dec, _ = x_sample.shape
    depth = ada_w.shape[0]
    assert depth == 2 and d == N_HEADS * HEAD_DIM
    n_pages = page_table.shape[1]
    page = cache_k.shape[2]
    past = n_pages * page
    kv = N_KV_HEADS * HEAD_DIM

    bc = -(-(b + bd) // SUBLANES) * SUBLANES
    c_all = _pad_axis(jnp.concatenate([c_prompt, c_sample], axis=0), 0, bc)
    mod = _modulation(c_all, ada_w.reshape(depth * 2, d, 3 * d), ada_b.reshape(depth * 2, 1, 3 * d))

    def mods(layer, sub, lo, n, repeat=None):
        m = mod[layer * 2 + sub, lo:lo + n]
        out = []
        for j in range(3):
            mj = m[:, None, j * d:(j + 1) * d]
            if repeat is not None:
                mj = jnp.broadcast_to(mj, (n, repeat, d)).reshape(1, n * repeat, d)
            out.append(mj)
        return out

    g_vec = lambda i, j: norm_g[i, j].reshape(1, d)
    pool_w_bf = pool_w[0].astype(BF16)
    pool_ps = pool_scale[0].reshape(1, d)
    win_bf = ffn_w_in.astype(BF16)
    wout_bf = ffn_w_out.astype(BF16)
    wo_bf = attn_w_o[0].astype(BF16)
    in_cols = attn_w_in.shape[-1]
    w_in_bf = _pad_axis(attn_w_in[0], 1, in_cols + (-in_cols % LANES)).astype(BF16)
    final_vec = final_g.reshape(1, d)

    tr = _row_tile(s, 512)
    sh, sc, gt = mods(0, 0, 0, b)
    x1, pool_p = _pool_layer(x_prompt, jnp.zeros((b, 16, d), F32), sh, sc, gt, g_vec(0, 0),
                             pool_w_bf, pool_ps, tr=tr, t_valid=tr, pos0=0)
    sh, sc, gt = mods(0, 1, 0, b)
    x2 = _ffn_layer(x1, sh, sc, gt, g_vec(0, 1), win_bf[0], wout_bf[0], tr=tr)
    sh, sc, gta = mods(1, 0, 0, b)
    k_p, v_p, ki_p, wit_p, q_hm, qi_hm, k_hm, vt_p, ki_bf = _dsa_project(
        x2, sh, sc, g_vec(1, 0), w_in_bf, tr=tr)
    att = _dsa_prompt(qi_hm, wit_p, q_hm, ki_bf, k_hm, vt_p)
    sh, sc, gt = mods(1, 1, 0, b)
    y_prompt = _ffn_layer(x2, sh, sc, gt, g_vec(1, 1), win_bf[1], wout_bf[1], tr=tr,
                          attn=(att, gta, wo_bf), final_g=final_vec)

    rs = bd * t_dec
    t_pad = -(-t_dec // SUBLANES) * SUBLANES
    sh, sc, gt = mods(0, 0, b, bd)
    prefix = jnp.pad(state_pool[0], ((0, 0), (16 - POOL_STATE, 0), (0, 0)))
    x1s, pool_s = _pool_layer(_pad_axis(x_sample, 1, t_pad), prefix, sh, sc, gt, g_vec(0, 0),
                              pool_w_bf, pool_ps, tr=t_pad, t_valid=t_dec, pos0=past)
    x1s = x1s[:, :t_dec].reshape(1, rs, d)
    sh, sc, gt = mods(0, 1, b, bd, repeat=t_dec)
    x2s = _ffn_layer(x1s, sh, sc, gt, g_vec(0, 1), win_bf[0], wout_bf[0], tr=rs)
    sh, sc, gta = mods(1, 0, b, bd, repeat=t_dec)
    k_s, v_s, ki_s, wit_s, q_s, qi_s, _, _, _ = _dsa_project(x2s, sh, sc, g_vec(1, 0), w_in_bf, tr=rs)

    qi_rows = qi_s[0].reshape(N_IDX_HEADS, bd, t_dec, IDX_DIM).transpose(1, 2, 0, 3)
    qi_rows = qi_rows.reshape(bd, t_dec * N_IDX_HEADS, IDX_DIM)
    wi_rows = wit_s[0].reshape(N_IDX_HEADS, bd, t_dec).transpose(1, 2, 0)
    wi_rows = wi_rows.reshape(bd, t_dec * N_IDX_HEADS, 1)
    q_rows = q_s[0].reshape(N_KV_HEADS, GQA_GROUP, bd, t_dec, HEAD_DIM).transpose(2, 0, 1, 3, 4)
    eye = jnp.eye(N_KV_HEADS, dtype=BF16)
    q_rows = q_rows[:, :, :, :, None, :] * eye[None, :, None, None, :, None]
    q_rows = q_rows.reshape(bd, N_HEADS * t_dec, kv)
    ki_new_t = _pad_axis(ki_s.reshape(bd, t_dec, IDX_DIM).transpose(0, 2, 1), 2, LANES).astype(BF16)
    new_t = lambda a: _pad_axis(a.reshape(bd, t_dec, kv).transpose(0, 2, 1), 2, LANES).astype(BF16)
    cache_k_t = cache_k[0].transpose(0, 2, 3, 1)
    cache_v_t = cache_v[0].transpose(0, 2, 3, 1)
    cache_ki_t = cache_kidx[0].transpose(0, 2, 1)

    pc = next(p for p in (32, 16, 8, 4, 2, 1) if n_pages % p == 0)
    i_past, i_new = _sample_scores(page_table, qi_rows, wi_rows, ki_new_t, cache_ki_t,
                                   pc=pc, t_dec=t_dec)
    n_keys = past + LANES
    n_keys_pad = -(-n_keys // SELECT_CHUNK) * SELECT_CHUNK
    scores_t = jnp.concatenate([i_past, i_new], axis=-1).reshape(rs, n_keys).T
    scores_t = _pad_axis(scores_t, 0, n_keys_pad, -jnp.inf)
    n_valid = (past + 1 + jnp.arange(rs, dtype=jnp.int32) % t_dec).astype(F32).reshape(1, rs)
    scores_t, thr = _sample_select(scores_t, n_valid, k_sel=min(TOPK_MAX, (past + t_dec) // 4))
    scores = scores_t[:n_keys].T.reshape(bd, t_dec, n_keys)
    att_s = _sample_attend(page_table, q_rows, thr.reshape(bd, t_dec, 1),
                           scores[:, :, :past], scores[:, :, past:], new_t(k_s), new_t(v_s),
                           cache_k_t, cache_v_t, pc=pc, t_dec=t_dec)
    att_s = att_s.reshape(bd, N_KV_HEADS, GQA_GROUP, t_dec, N_KV_HEADS, HEAD_DIM)
    att_s = jnp.stack([att_s[:, n, :, :, n, :] for n in range(N_KV_HEADS)], axis=1)
    att_s = att_s.transpose(0, 3, 1, 2, 4).reshape(1, rs, N_HEADS * HEAD_DIM).astype(BF16)
    sh, sc, gt = mods(1, 1, b, bd, repeat=t_dec)
    y_sample = _ffn_layer(x2s, sh, sc, gt, g_vec(1, 1), win_bf[1], wout_bf[1], tr=rs,
                          attn=(att_s, gta, wo_bf), final_g=final_vec)

    return (y_prompt,
            y_sample.reshape(bd, t_dec, d),
            pool_p[None],
            k_p.reshape(1, b, s, N_KV_HEADS, HEAD_DIM),
            v_p.reshape(1, b, s, N_KV_HEADS, HEAD_DIM),
            ki_p[None],
            pool_s[None],
            k_s.reshape(1, bd, t_dec, N_KV_HEADS, HEAD_DIM),
            v_s.reshape(1, bd, t_dec, N_KV_HEADS, HEAD_DIM),
            ki_s.reshape(1, bd, t_dec, IDX_DIM))
```

```python
import functools

import jax
import jax.numpy as jnp
from jax import lax
from jax.experimental import pallas as pl
from jax.experimental.pallas import tpu as pltpu

F32 = jnp.float32
BF16 = jnp.bfloat16

N_HEADS = 16
N_KV_HEADS = 4
GQA_GROUP = N_HEADS // N_KV_HEADS
HEAD_DIM = 64
N_IDX_HEADS = 8
IDX_DIM = 64
TOPK_MAX = 256
POOL_WINDOWS = (2, 4, 8, 16)
POOL_STATE = max(POOL_WINDOWS) - 1
RMS_EPS = 1e-6

LANES = 128
SUBLANES = 8
VMEM_LIMIT_BYTES = 56 * 1024 * 1024
MASKED_SCORE = -1e30
SOFTMAX_SUM_FLOOR = 2.0 ** -100
DENORMAL_KEYS = 0x007FFFFF
KEY_LOWEST = -2139095040 + DENORMAL_KEYS
KEY_POS_INF = 0x7F800000 - DENORMAL_KEYS
LOG2_E = 1.4426950408889634
SELECT_MAX_ITERS = 40
PROBE_SPAN_KEYS = 4 << 23
PASSES_PER_CHECK = 2
SELECT_CHUNK = 1024
COUNT_ACC_ROWS = 32
POOL_HALO = 24
VT_ROWS = HEAD_DIM + 16
NT_DIMS = (((1,), (1,)), ((), ()))


def _cparams(*semantics):
    return pltpu.CompilerParams(dimension_semantics=semantics,
                                vmem_limit_bytes=VMEM_LIMIT_BYTES)


def _silu(x):
    return x * jax.nn.sigmoid(x)


def _normmod(x, g, sc, sh):
    ms = jnp.mean(x * x, axis=-1, keepdims=True)
    return x * lax.rsqrt(ms + RMS_EPS) * g * (1.0 + sc) + sh


def _resident(shape):
    nd = len(shape)
    return pl.BlockSpec(shape, lambda *_: (0,) * nd)


def _mod_kernel(c_ref, w_ref, b_ref, o_ref):
    a = _silu(c_ref[...]).astype(BF16)
    o_ref[0] = jnp.dot(a, w_ref[0].astype(BF16), preferred_element_type=F32) + b_ref[0]


def _modulation(c_all, ada_w, ada_b):
    n_l, d, d3 = ada_w.shape
    bc = c_all.shape[0]
    tn = d3 // 2
    return pl.pallas_call(
        _mod_kernel,
        grid=(n_l, d3 // tn),
        in_specs=[pl.BlockSpec((bc, d), lambda l, j: (0, 0)),
                  pl.BlockSpec((1, d, tn), lambda l, j: (l, 0, j)),
                  pl.BlockSpec((1, 1, tn), lambda l, j: (l, 0, j))],
        out_specs=pl.BlockSpec((1, bc, tn), lambda l, j: (l, 0, j)),
        out_shape=jax.ShapeDtypeStruct((n_l, bc, d3), F32),
        compiler_params=_cparams("parallel", "parallel"),
        name="modulation",
    )(c_all, ada_w, ada_b)


def _pool_kernel(x_ref, pre_ref, sh_ref, sc_ref, gt_ref, g_ref, w_ref, ps_ref,
                 o_ref, st_ref, e_ref, b1_ref, b2_ref, b3_ref, *, tr, t_valid, pos0):
    t = pl.program_id(1)
    d = x_ref.shape[-1]
    grp = d // len(POOL_WINDOWS)
    top = tr + POOL_HALO

    @pl.when(t == 0)
    def _():
        zeros = jnp.zeros((SUBLANES, d), F32)
        e_ref[0:SUBLANES, :] = zeros
        b1_ref[0:SUBLANES, :] = zeros
        b2_ref[0:SUBLANES, :] = zeros
        e_ref[SUBLANES:POOL_HALO, :] = pre_ref[0]

    x = x_ref[0]
    h = _normmod(x, g_ref[...], sc_ref[0], sh_ref[0])
    e_ref[POOL_HALO:top, :] = h
    b1_ref[SUBLANES:top, :] = e_ref[SUBLANES:top, :] + e_ref[SUBLANES - 1:top - 1, :]
    b2_ref[SUBLANES:top, grp:] = b1_ref[SUBLANES:top, grp:] + b1_ref[SUBLANES - 2:top - 2, grp:]
    b3_ref[SUBLANES:top, 2 * grp:] = (b2_ref[SUBLANES:top, 2 * grp:]
                                       + b2_ref[SUBLANES - 4:top - 4, 2 * grp:])
    sums = (b1_ref[POOL_HALO:top, 0:grp],
            b2_ref[POOL_HALO:top, grp:2 * grp],
            b3_ref[POOL_HALO:top, 2 * grp:3 * grp],
            b3_ref[POOL_HALO:top, 3 * grp:] + b3_ref[POOL_HALO - 8:top - 8, 3 * grp:])

    pos = pos0 + t * tr + lax.broadcasted_iota(jnp.int32, (tr, 1), 0)
    ys = []
    for gi, w in enumerate(POOL_WINDOWS):
        inv_cnt = 1.0 / jnp.minimum(w, pos + 1).astype(F32)
        diff = sums[gi] * inv_cnt - h[:, gi * grp:(gi + 1) * grp]
        ys.append(jnp.dot(diff.astype(BF16), w_ref[gi], preferred_element_type=F32))
    y = jnp.concatenate(ys, axis=-1) * ps_ref[...]
    o_ref[0] = x + gt_ref[0] * y

    st_ref[0] = e_ref[POOL_HALO + t_valid - POOL_STATE:POOL_HALO + t_valid, :]
    e_ref[SUBLANES:POOL_HALO, :] = e_ref[tr + SUBLANES:top, :]


def _pool_layer(x, prefix, sh, sc, gt, g, w_bf, ps, *, tr, t_valid, pos0):
    b, r, d = x.shape
    grp = d // len(POOL_WINDOWS)
    nt = r // tr
    row = pl.BlockSpec((1, tr, d), lambda i, j: (i, j, 0))
    per_b = pl.BlockSpec((1, 1, d), lambda i, j: (i, 0, 0))
    vec = pl.BlockSpec((1, d), lambda i, j: (0, 0))
    rows = tr + POOL_HALO
    return pl.pallas_call(
        functools.partial(_pool_kernel, tr=tr, t_valid=t_valid, pos0=pos0),
        grid=(b, nt),
        in_specs=[row,
                  pl.BlockSpec((1, 16, d), lambda i, j: (i, 0, 0)),
                  per_b, per_b, per_b, vec,
                  _resident((len(POOL_WINDOWS), grp, grp)),
                  vec],
        out_specs=[row, pl.BlockSpec((1, POOL_STATE, d), lambda i, j: (i, 0, 0))],
        out_shape=[jax.ShapeDtypeStruct((b, r, d), F32),
                   jax.ShapeDtypeStruct((b, POOL_STATE, d), F32)],
        scratch_shapes=[pltpu.VMEM((rows, d), F32)] * 4,
        compiler_params=_cparams("parallel", "arbitrary"),
        name="pool_layer",
    )(x, prefix, sh, sc, gt, g, w_bf, ps)


def _ffn_kernel(*refs, has_attn, has_final, n_chunks):
    it = iter(refs)
    x_ref = next(it)
    if has_attn:
        att_ref, gta_ref, wo_ref = next(it), next(it), next(it)
    sh_ref, sc_ref, gt_ref, g_ref, win_ref, wout_ref = (next(it) for _ in range(6))
    if has_final:
        fg_ref = next(it)
    o_ref = next(it)

    x = x_ref[0]
    if has_attn:
        x = x + gta_ref[0] * jnp.dot(att_ref[0], wo_ref[...], preferred_element_type=F32)
    h = _normmod(x, g_ref[...], sc_ref[0], sh_ref[0]).astype(BF16)
    f = wout_ref.shape[0]
    fc = f // n_chunks
    acc = jnp.zeros(x.shape, F32)
    for c in range(n_chunks):
        gate = jnp.dot(h, win_ref[:, c * fc:(c + 1) * fc], preferred_element_type=F32)
        up = jnp.dot(h, win_ref[:, f + c * fc:f + (c + 1) * fc], preferred_element_type=F32)
        act = (_silu(gate) * up).astype(BF16)
        acc = acc + jnp.dot(act, wout_ref[c * fc:(c + 1) * fc, :], preferred_element_type=F32)
    y = x + gt_ref[0] * acc
    if has_final:
        ms = jnp.mean(y * y, axis=-1, keepdims=True)
        y = y * lax.rsqrt(ms + RMS_EPS) * fg_ref[...]
    o_ref[0] = y


def _mod_spec(m, tr):
    d = m.shape[-1]
    if m.shape[1] == 1:
        return pl.BlockSpec((1, 1, d), lambda i, j: (i, 0, 0))
    return pl.BlockSpec((1, tr, d), lambda i, j: (i, j, 0))


def _ffn_layer(x, sh, sc, gt, g, win_bf, wout_bf, *, tr, attn=None, final_g=None):
    b, r, d = x.shape
    f = wout_bf.shape[0]
    row = pl.BlockSpec((1, tr, d), lambda i, j: (i, j, 0))
    vec = pl.BlockSpec((1, d), lambda i, j: (0, 0))
    args, specs = [x], [row]
    if attn is not None:
        att, gta, wo_bf = attn
        args += [att, gta, wo_bf]
        specs += [pl.BlockSpec((1, tr, att.shape[-1]), lambda i, j: (i, j, 0)),
                  _mod_spec(gta, tr), _resident(wo_bf.shape)]
    args += [sh, sc, gt, g, win_bf, wout_bf]
    specs += [_mod_spec(sh, tr), _mod_spec(sc, tr), _mod_spec(gt, tr), vec,
              _resident(win_bf.shape), _resident(wout_bf.shape)]
    if final_g is not None:
        args.append(final_g)
        specs.append(vec)
    n_chunks = 2 if f % (2 * LANES) == 0 else 1
    return pl.pallas_call(
        functools.partial(_ffn_kernel, has_attn=attn is not None,
                          has_final=final_g is not None, n_chunks=n_chunks),
        grid=(b, r // tr),
        in_specs=specs,
        out_specs=row,
        out_shape=jax.ShapeDtypeStruct((b, r, d), F32),
        compiler_params=_cparams("parallel", "parallel"),
        name="ffn_layer",
    )(*args)


def _proj_kernel(x_ref, sh_ref, sc_ref, g_ref, w_ref,
                 k_ref, v_ref, ki_ref, wit_ref, q_ref, qi_ref, kh_ref, vt_ref, kib_ref):
    tr = x_ref.shape[1]
    kv = N_KV_HEADS * HEAD_DIM
    o_k = N_HEADS * HEAD_DIM
    o_v = o_k + kv
    o_qi = o_v + kv
    o_ki = o_qi + N_IDX_HEADS * IDX_DIM
    h = _normmod(x_ref[0], g_ref[...], sc_ref[0], sh_ref[0]).astype(BF16)
    p = jnp.dot(h, w_ref[...], preferred_element_type=F32)
    k_ref[0] = p[:, o_k:o_v]
    v_ref[0] = p[:, o_v:o_qi]
    ki = p[:, o_ki:o_ki + IDX_DIM]
    ki_ref[0] = ki
    kib_ref[0] = ki.astype(BF16)
    tail_t = p[:, o_ki:o_ki + LANES].T
    wit_ref[0] = tail_t[IDX_DIM:IDX_DIM + N_IDX_HEADS] * (N_IDX_HEADS ** -0.5 * IDX_DIM ** -0.5)
    for hd in range(N_HEADS):
        q_ref[0, hd] = (p[:, hd * HEAD_DIM:(hd + 1) * HEAD_DIM]
                        * (HEAD_DIM ** -0.5 * LOG2_E)).astype(BF16)
    for hd in range(N_IDX_HEADS):
        qi_ref[0, hd] = p[:, o_qi + hd * IDX_DIM:o_qi + (hd + 1) * IDX_DIM].astype(BF16)
    ones_pad = (lax.broadcasted_iota(jnp.int32, (tr, LANES - HEAD_DIM), 1) == 0).astype(F32)
    for n in range(N_KV_HEADS):
        kh_ref[0, n] = p[:, o_k + n * HEAD_DIM:o_k + (n + 1) * HEAD_DIM].astype(BF16)
        vn = p[:, o_v + n * HEAD_DIM:o_v + (n + 1) * HEAD_DIM]
        vt_ref[0, n] = jnp.concatenate([vn, ones_pad], axis=-1).T[0:VT_ROWS].astype(BF16)


def _dsa_project(x, sh, sc, g, w_bf, *, tr):
    b, r, d = x.shape
    kv = N_KV_HEADS * HEAD_DIM
    row = lambda n: pl.BlockSpec((1, tr, n), lambda i, j: (i, j, 0))
    heads = lambda nh, n: pl.BlockSpec((1, nh, tr, n), lambda i, j: (i, 0, j, 0))
    sds = jax.ShapeDtypeStruct
    return pl.pallas_call(
        _proj_kernel,
        grid=(b, r // tr),
        in_specs=[row(d), _mod_spec(sh, tr), _mod_spec(sc, tr),
                  pl.BlockSpec((1, d), lambda i, j: (0, 0)), _resident(w_bf.shape)],
        out_specs=[row(kv), row(kv), row(IDX_DIM),
                   pl.BlockSpec((1, N_IDX_HEADS, tr), lambda i, j: (i, 0, j)),
                   heads(N_HEADS, HEAD_DIM), heads(N_IDX_HEADS, IDX_DIM),
                   heads(N_KV_HEADS, HEAD_DIM),
                   pl.BlockSpec((1, N_KV_HEADS, VT_ROWS, tr), lambda i, j: (i, 0, 0, j)),
                   row(IDX_DIM)],
        out_shape=[sds((b, r, kv), F32), sds((b, r, kv), F32), sds((b, r, IDX_DIM), F32),
                   sds((b, N_IDX_HEADS, r), F32),
                   sds((b, N_HEADS, r, HEAD_DIM), BF16), sds((b, N_IDX_HEADS, r, IDX_DIM), BF16),
                   sds((b, N_KV_HEADS, r, HEAD_DIM), BF16), sds((b, N_KV_HEADS, VT_ROWS, r), BF16),
                   sds((b, r, IDX_DIM), BF16)],
        compiler_params=_cparams("parallel", "parallel"),
        name="dsa_project",
    )(x, sh, sc, g, w_bf)


def _key_to_f32(k):
    k = jnp.where(k > 0, k + DENORMAL_KEYS, jnp.where(k < -1, k - DENORMAL_KEYS, k))
    return lax.bitcast_convert_type(jnp.where(k >= 0, k, k ^ 0x7FFFFFFF), F32)


def _f32_to_key(v):
    b = lax.bitcast_convert_type(v, jnp.int32)
    k = jnp.where(b >= 0, b, b ^ 0x7FFFFFFF)
    return jnp.where(k > DENORMAL_KEYS, k - DENORMAL_KEYS,
                     jnp.where(k < -1 - DENORMAL_KEYS, k + DENORMAL_KEYS,
                               jnp.where(k >= 0, 0, -1)))


def _count(s_ref, n_chunks, rc, pred):
    q = s_ref.shape[1]
    ar = min(rc, COUNT_ACC_ROWS)

    def body(c, acc):
        r0 = pl.multiple_of(c * rc, rc)
        hit = jnp.where(pred(s_ref[pl.ds(r0, rc), :], r0), 1.0, 0.0)
        return acc + jnp.sum(hit.reshape(rc // ar, ar, q), axis=0)

    acc = lax.fori_loop(0, n_chunks, body, jnp.zeros((ar, q), F32))
    return jnp.sum(acc, axis=0, keepdims=True)


def _select_threshold(s_ref, n_chunks, rc, n_valid, k_sel, top=None):
    q = s_ref.shape[1]
    kf = float(k_sel)
    if top is not None:
        top_key = _f32_to_key(top)

    def midpoint(lo, hi):
        return jnp.where(jnp.logical_and(lo == 0, hi > 1), 1, (lo & hi) + ((lo ^ hi) >> 1))

    def cond(st):
        return jnp.logical_and(st[0] < SELECT_MAX_ITERS, st[1] > 0.0)

    def body(st):
        for _ in range(PASSES_PER_CHECK):
            st = one_pass(st)
        _, _, lo, hi, c_lo, _ = st
        still = jnp.logical_and(c_lo > kf, midpoint(lo, hi) != lo)
        return (st[0], jnp.sum(jnp.where(still, 1.0, 0.0))) + st[2:]

    def one_pass(st):
        it, _, lo, hi, c_lo, c_hi = st
        mid = midpoint(lo, hi)
        active = jnp.logical_and(c_lo > kf, mid != lo)
        if top is not None:
            probe = jnp.where(it == 0, top_key, top_key - PROBE_SPAN_KEYS)
            use = jnp.logical_and(it < 2, jnp.logical_and(probe > lo, probe < hi))
            mid = jnp.where(use, probe, mid)
        thr = _key_to_f32(mid)
        c = _count(s_ref, n_chunks, rc, lambda x, r0: x >= thr)
        ge = c >= kf
        up = jnp.logical_and(active, ge)
        dn = jnp.logical_and(active, jnp.logical_not(ge))
        lo = jnp.where(up, mid, lo)
        c_lo = jnp.where(up, c, c_lo)
        hi = jnp.where(dn, mid, hi)
        c_hi = jnp.where(dn, c, c_hi)
        return it + 1, st[1], lo, hi, c_lo, c_hi

    st = lax.while_loop(cond, body,
                        (jnp.int32(0), jnp.sum(jnp.where(n_valid > kf, 1.0, 0.0)),
                         jnp.full((1, q), KEY_LOWEST, jnp.int32), jnp.full((1, q), KEY_POS_INF, jnp.int32),
                         n_valid, jnp.zeros((1, q), F32)))
    _, _, lo, _, c_lo, c_hi = st
    thr = _key_to_f32(lo)
    tied = c_lo > kf
    need = kf - c_hi

    @pl.when(jnp.sum(jnp.where(tied, 1.0, 0.0)) > 0.0)
    def _():
        def key_ids(r0, shape):
            return r0 + lax.broadcasted_iota(jnp.int32, shape, 0)

        def jbody(_, st):
            jlo, jhi = st
            jm = (jlo + jhi) >> 1
            c = _count(s_ref, n_chunks, rc,
                       lambda x, r0: jnp.logical_and(x == thr, key_ids(r0, x.shape) <= jm))
            ok = c >= need
            return jnp.where(ok, jlo, jm), jnp.where(ok, jm, jhi)

        n_it = max(1, (s_ref.shape[0] - 1).bit_length())
        _, jcut = lax.fori_loop(0, n_it, jbody,
                                (jnp.full((1, q), -1, jnp.int32),
                                 jnp.zeros((1, q), jnp.int32) + (n_chunks * rc - 1)))

        def drop(c, carry):
            r0 = pl.multiple_of(c * rc, rc)
            x = s_ref[pl.ds(r0, rc), :]
            kill = jnp.logical_and(jnp.logical_and(tied, x == thr), key_ids(r0, x.shape) > jcut)
            s_ref[pl.ds(r0, rc), :] = jnp.where(kill, -jnp.inf, x)
            return carry

        lax.fori_loop(0, n_chunks, drop, 0)

    return thr


def _dsa_prompt_kernel(qi_ref, wit_ref, q_ref, ki_ref, k_ref, vt_ref, o_ref,
                       s_ref, m_ref, acc_ref, sa_ref, sb_ref, ta_ref, tb_ref, pa_ref, pb_ref,
                       *, tq, tk, ts, fill, k_sel):
    s_len = s_ref.shape[0]
    q0 = pl.program_id(1) * tq
    n_fill = (q0 + tq + fill - 1) // fill
    q_ids = q0 + lax.broadcasted_iota(jnp.int32, (1, tq), 1)
    gq = GQA_GROUP * tq

    def scores(kf, top):
        for sub in range(fill // ts):
            top = score_tile(kf * (fill // ts) + sub, top)
        return top

    def score_tile(kt, top):
        r0 = pl.multiple_of(kt * ts, ts)
        ki = ki_ref[0, pl.ds(r0, ts), :]
        acc = jnp.zeros((ts, tq), F32)
        for j in range(N_IDX_HEADS // 2):
            qi2 = qi_ref[0, 2 * j:2 * j + 2].reshape(2 * tq, IDX_DIM)
            s2 = lax.dot_general(ki, qi2, NT_DIMS, preferred_element_type=F32)
            for u in range(2):
                hd = 2 * j + u
                acc = acc + jnp.maximum(s2[:, u * tq:(u + 1) * tq], 0.0) * wit_ref[0, hd:hd + 1, :]
        k_ids = r0 + lax.broadcasted_iota(jnp.int32, (ts, 1), 0)
        visible = jnp.where(k_ids <= q_ids, acc, -jnp.inf)
        s_ref[pl.ds(r0, ts), :] = visible
        return jnp.maximum(top, jnp.max(visible, axis=0, keepdims=True))

    top = lax.fori_loop(0, n_fill, scores, jnp.full((1, tq), -jnp.inf, F32))

    thr = _select_threshold(s_ref, n_fill, fill, (q_ids + 1).astype(F32), k_sel, top=top)

    def raw_scores(kt, dst_ref, top_ref):
        r0 = pl.multiple_of(jnp.minimum(kt * tk, s_len - tk), tk)
        for n in range(N_KV_HEADS):
            qn = q_ref[0, n * GQA_GROUP:(n + 1) * GQA_GROUP].reshape(gq, HEAD_DIM)
            t = lax.dot_general(k_ref[0, n, pl.ds(r0, tk), :], qn, NT_DIMS,
                                preferred_element_type=F32)
            dst_ref[n] = t
            top_ref[n] = jnp.max(t, axis=0, keepdims=True)

    def pv(kt, p_ref, n):
        r0 = pl.multiple_of(jnp.maximum(kt, 0) * tk, tk)
        return jnp.dot(vt_ref[0, n, :, pl.ds(r0, tk)], p_ref[n], preferred_element_type=F32)

    def step(kt, cur, nxt, p_cur, p_prv, masked_max):
        s_cur, top_cur = cur
        raw_scores(kt + 1, *nxt)
        r0 = pl.multiple_of(kt * tk, tk)
        keep = s_ref[pl.ds(r0, tk), :] >= thr
        if masked_max:
            bias = jnp.where(keep, 0.0, MASKED_SCORE)
        else:
            keep01 = jnp.where(keep, 1.0, 0.0).astype(BF16)
        for n in range(N_KV_HEADS):
            pv_n = pv(kt - 1, p_prv, n)
            for g in range(GQA_GROUP):
                cols = slice(g * tq, (g + 1) * tq)
                s = s_cur[n, :, cols]
                if masked_max:
                    s = s + bias
                m_old = m_ref[n, :, cols]
                tile_top = jnp.max(s, axis=0, keepdims=True) if masked_max else top_cur[n, :, cols]
                m_new = jnp.maximum(m_old, tile_top)
                if masked_max:
                    p_cur[n, :, cols] = jnp.exp2(s - m_new).astype(BF16)
                else:
                    p_cur[n, :, cols] = jnp.exp2((s - m_new).astype(BF16)) * keep01
                acc_ref[n, :, cols] = jnp.exp2(m_old - m_new) * (acc_ref[n, :, cols] + pv_n[:, cols])
                m_ref[n, :, cols] = m_new

    n_pairs = (q0 + tq + 2 * tk - 1) // (2 * tk)

    def attention(masked_max):
        m_ref[...] = jnp.full(m_ref.shape, MASKED_SCORE, F32)
        acc_ref[...] = jnp.zeros(acc_ref.shape, F32)
        pb_ref[...] = jnp.zeros(pb_ref.shape, BF16)
        even, odd = (sa_ref, ta_ref), (sb_ref, tb_ref)
        raw_scores(0, *even)

        def pair(i, carry):
            step(2 * i, even, odd, pa_ref, pb_ref, masked_max)
            step(2 * i + 1, odd, even, pb_ref, pa_ref, masked_max)
            return carry

        lax.fori_loop(0, n_pairs, pair, 0)
        for n in range(N_KV_HEADS):
            acc_ref[n] = acc_ref[n] + pv(2 * n_pairs - 1, pb_ref, n)

    attention(masked_max=False)
    sums = acc_ref[:, HEAD_DIM:HEAD_DIM + 1, :]
    underflowed = jnp.sum(jnp.where(sums >= SOFTMAX_SUM_FLOOR, 0.0, 1.0)) > 0.0

    @pl.when(underflowed)
    def _():
        attention(masked_max=True)

    for n in range(N_KV_HEADS):
        a = acc_ref[n]
        a = jnp.concatenate([a, jnp.zeros((LANES - VT_ROWS, gq), F32)], axis=0).T
        o = a[:, 0:HEAD_DIM] / a[:, HEAD_DIM:HEAD_DIM + 1]
        for g in range(GQA_GROUP):
            hd = n * GQA_GROUP + g
            o_ref[0, :, hd * HEAD_DIM:(hd + 1) * HEAD_DIM] = o[g * tq:(g + 1) * tq].astype(o_ref.dtype)


def _dsa_prompt(qi_hm, wit, q_hm, ki_bf, k_hm, vt):
    b, _, s, _ = q_hm.shape
    k_sel = min(TOPK_MAX, s // 4)
    tq = _row_tile(s, 128)
    tk = _row_tile(s, 256)
    ts = _row_tile(s, 512)
    fill = _row_tile(s, SELECT_CHUNK)
    assert fill % (2 * tk) == 0 and fill % ts == 0
    gq = GQA_GROUP * tq
    return pl.pallas_call(
        functools.partial(_dsa_prompt_kernel, tq=tq, tk=tk, ts=ts, fill=fill, k_sel=k_sel),
        grid=(b, s // tq),
        in_specs=[pl.BlockSpec((1, N_IDX_HEADS, tq, IDX_DIM), lambda i, j: (i, 0, j, 0)),
                  pl.BlockSpec((1, N_IDX_HEADS, tq), lambda i, j: (i, 0, j)),
                  pl.BlockSpec((1, N_HEADS, tq, HEAD_DIM), lambda i, j: (i, 0, j, 0)),
                  pl.BlockSpec((1, s, IDX_DIM), lambda i, j: (i, 0, 0)),
                  pl.BlockSpec((1, N_KV_HEADS, s, HEAD_DIM), lambda i, j: (i, 0, 0, 0)),
                  pl.BlockSpec((1, N_KV_HEADS, VT_ROWS, s), lambda i, j: (i, 0, 0, 0))],
        out_specs=pl.BlockSpec((1, tq, N_HEADS * HEAD_DIM), lambda i, j: (i, j, 0)),
        out_shape=jax.ShapeDtypeStruct((b, s, N_HEADS * HEAD_DIM), BF16),
        scratch_shapes=[pltpu.VMEM((s, tq), F32),
                        pltpu.VMEM((N_KV_HEADS, 1, gq), F32),
                        pltpu.VMEM((N_KV_HEADS, VT_ROWS, gq), F32),
                        pltpu.VMEM((N_KV_HEADS, tk, gq), F32),
                        pltpu.VMEM((N_KV_HEADS, tk, gq), F32),
                        pltpu.VMEM((N_KV_HEADS, 1, gq), F32),
                        pltpu.VMEM((N_KV_HEADS, 1, gq), F32),
                        pltpu.VMEM((N_KV_HEADS, tk, gq), BF16),
                        pltpu.VMEM((N_KV_HEADS, tk, gq), BF16)],
        compiler_params=_cparams("parallel", "arbitrary"),
        name="dsa_prompt",
    )(qi_hm, wit, q_hm, ki_bf, k_hm, vt)


def _page_specs(page_shape, pc):
    zeros = (0,) * len(page_shape)

    def spec(j):
        return pl.BlockSpec((1,) + page_shape, lambda b, c, pt: (pt[b, c * pc + j],) + zeros)
    return [spec(j) for j in range(pc)]


def _sample_scores_kernel(pt_ref, qi_ref, wi_ref, kin_ref, *rest, pc, t_dec):
    pages, (ip_ref, in_ref, kb_ref) = rest[:pc], rest[pc:]
    c = pl.program_id(1)
    qi = qi_ref[0]
    wi = wi_ref[0]
    page = pages[0].shape[2]

    def score(ki_t):
        s = jnp.dot(qi, ki_t, preferred_element_type=F32)
        s = jnp.maximum(s, 0.0) * wi
        return jnp.sum(s.reshape(t_dec, N_IDX_HEADS, s.shape[-1]), axis=1)

    for j in range(pc):
        kb_ref[:, j * page:(j + 1) * page] = pages[j][0].astype(BF16)
    ip_ref[0] = score(kb_ref[...])

    @pl.when(c == pl.num_programs(1) - 1)
    def _():
        s = score(kin_ref[0])
        col = lax.broadcasted_iota(jnp.int32, s.shape, 1)
        row = lax.broadcasted_iota(jnp.int32, s.shape, 0)
        in_ref[0] = jnp.where(col <= row, s, -jnp.inf)


def _sample_scores(page_table, qi_s, wi_s, ki_new_t, cache_ki_t, *, pc, t_dec):
    bd, n_pages = page_table.shape
    page = cache_ki_t.shape[2]
    rows = qi_s.shape[1]
    grid_spec = pltpu.PrefetchScalarGridSpec(
        num_scalar_prefetch=1,
        grid=(bd, n_pages // pc),
        in_specs=[pl.BlockSpec((1, rows, IDX_DIM), lambda b, c, pt: (b, 0, 0)),
                  pl.BlockSpec((1, rows, 1), lambda b, c, pt: (b, 0, 0)),
                  pl.BlockSpec((1, IDX_DIM, LANES), lambda b, c, pt: (b, 0, 0))]
                 + _page_specs((IDX_DIM, page), pc),
        out_specs=[pl.BlockSpec((1, t_dec, pc * page), lambda b, c, pt: (b, 0, c)),
                   pl.BlockSpec((1, t_dec, LANES), lambda b, c, pt: (b, 0, 0))],
        scratch_shapes=[pltpu.VMEM((IDX_DIM, pc * page), BF16)],
    )
    return pl.pallas_call(
        functools.partial(_sample_scores_kernel, pc=pc, t_dec=t_dec),
        grid_spec=grid_spec,
        out_shape=[jax.ShapeDtypeStruct((bd, t_dec, n_pages * page), F32),
                   jax.ShapeDtypeStruct((bd, t_dec, LANES), F32)],
        compiler_params=_cparams("parallel", "arbitrary"),
        name="sample_scores",
    )(page_table, qi_s, wi_s, ki_new_t, *([cache_ki_t] * pc))


def _sample_select_kernel(s_ref, nv_ref, o_ref, thr_ref, *, k_sel):
    o_ref[...] = s_ref[...]
    thr_ref[...] = _select_threshold(o_ref, o_ref.shape[0] // SELECT_CHUNK, SELECT_CHUNK,
                                     nv_ref[...], k_sel)


def _sample_select(scores_t, n_valid, *, k_sel):
    r, q = scores_t.shape
    return pl.pallas_call(
        functools.partial(_sample_select_kernel, k_sel=k_sel),
        out_shape=[jax.ShapeDtypeStruct((r, q), F32), jax.ShapeDtypeStruct((1, q), F32)],
        compiler_params=pltpu.CompilerParams(vmem_limit_bytes=VMEM_LIMIT_BYTES),
        name="sample_select",
    )(scores_t, n_valid)


def _sample_attend_kernel(pt_ref, q_ref, thr_ref, ip_ref, in_ref, kn_ref, vn_ref, *rest, pc, t_dec):
    k_pages, v_pages = rest[:pc], rest[pc:2 * pc]
    o_ref, kb_ref, vb_ref, m_ref, l_ref, acc_ref = rest[2 * pc:]
    c = pl.program_id(1)
    page = k_pages[0].shape[3]
    rows, kv = q_ref.shape[1], q_ref.shape[2]

    @pl.when(c == 0)
    def _():
        m_ref[...] = jnp.full(m_ref.shape, MASKED_SCORE, F32)
        l_ref[...] = jnp.zeros(l_ref.shape, F32)
        acc_ref[...] = jnp.zeros(acc_ref.shape, F32)

    def update(k_t, v_t, scores):
        bias = jnp.where(scores >= thr_ref[0], 0.0, MASKED_SCORE)
        if SUBLANES % t_dec == 0:
            bias = jnp.concatenate([bias] * (SUBLANES // t_dec), axis=0)
        bias = jnp.concatenate([bias] * (rows // bias.shape[0]), axis=0)
        s = jnp.dot(q_ref[0], k_t, preferred_element_type=F32) + bias
        m_old = m_ref[...]
        m_new = jnp.maximum(m_old, jnp.max(s, axis=1, keepdims=True))
        p = jnp.exp2(s - m_new).astype(BF16)
        alpha = jnp.exp2(m_old - m_new)
        l_ref[...] = alpha * l_ref[...] + jnp.sum(p.astype(F32), axis=1, keepdims=True)
        acc_ref[...] = alpha * acc_ref[...] + lax.dot_general(p, v_t, NT_DIMS,
                                                              preferred_element_type=F32)
        m_ref[...] = m_new

    for j in range(pc):
        kb_ref[:, j * page:(j + 1) * page] = k_pages[j][0].reshape(kv, page).astype(BF16)
        vb_ref[:, j * page:(j + 1) * page] = v_pages[j][0].reshape(kv, page).astype(BF16)
    update(kb_ref[...], vb_ref[...], ip_ref[0])

    @pl.when(c == pl.num_programs(1) - 1)
    def _():
        update(kn_ref[0], vn_ref[0], in_ref[0])
        o_ref[0] = acc_ref[...] / l_ref[...]


def _sample_attend(page_table, q_s, thr, i_past, i_new, k_new_t, v_new_t, cache_k_t, cache_v_t,
                   *, pc, t_dec):
    bd, n_pages = page_table.shape
    page = cache_k_t.shape[3]
    rows, kv = q_s.shape[1], q_s.shape[2]
    per_b = lambda shape: pl.BlockSpec((1,) + shape, lambda b, c, pt: (b, 0, 0))
    grid_spec = pltpu.PrefetchScalarGridSpec(
        num_scalar_prefetch=1,
        grid=(bd, n_pages // pc),
        in_specs=[per_b((rows, kv)),
                  per_b((t_dec, 1)),
                  pl.BlockSpec((1, t_dec, pc * page), lambda b, c, pt: (b, 0, c)),
                  per_b((t_dec, LANES)),
                  per_b((kv, LANES)),
                  per_b((kv, LANES))]
                 + _page_specs((N_KV_HEADS, HEAD_DIM, page), pc) * 2,
        out_specs=per_b((rows, kv)),
        scratch_shapes=[pltpu.VMEM((kv, pc * page), BF16),
                        pltpu.VMEM((kv, pc * page), BF16),
                        pltpu.VMEM((rows, 1), F32),
                        pltpu.VMEM((rows, 1), F32),
                        pltpu.VMEM((rows, kv), F32)],
    )
    return pl.pallas_call(
        functools.partial(_sample_attend_kernel, pc=pc, t_dec=t_dec),
        grid_spec=grid_spec,
        out_shape=jax.ShapeDtypeStruct((bd, rows, kv), F32),
        compiler_params=_cparams("parallel", "arbitrary"),
        name="sample_attend",
    )(page_table, q_s, thr, i_past, i_new, k_new_t, v_new_t,
      *([cache_k_t] * pc), *([cache_v_t] * pc))


def _row_tile(r, cap):
    tr = min(r, cap)
    assert r % tr == 0, (r, tr)
    return tr


def _pad_axis(a, axis, n, value=0.0):
    pad = [(0, 0)] * a.ndim
    pad[axis] = (0, n - a.shape[axis])
    return jnp.pad(a, pad, constant_values=value)


def kernel(x_prompt, x_sample, c_prompt, c_sample, state_pool, cache_k, cache_v, cache_kidx, page_table,
           ada_w, ada_b, norm_g, pool_w, pool_scale, attn_w_in, attn_w_o, ffn_w_in, ffn_w_out, final_g):
    b, s, d = x_prompt.shape
    bd, t_dec, _ = x_sample.shape
    depth = ada_w.shape[0]
    assert depth == 2 and d == N_HEADS * HEAD_DIM
    n_pages = page_table.shape[1]
    page = cache_k.shape[2]
    past = n_pages * page
    kv = N_KV_HEADS * HEAD_DIM

    bc = -(-(b + bd) // SUBLANES) * SUBLANES
    c_all = _pad_axis(jnp.concatenate([c_prompt, c_sample], axis=0), 0, bc)
    mod = _modulation(c_all, ada_w.reshape(depth * 2, d, 3 * d), ada_b.reshape(depth * 2, 1, 3 * d))

    def mods(layer, sub, lo, n, repeat=None):
        m = mod[layer * 2 + sub, lo:lo + n]
        out = []
        for j in range(3):
            mj = m[:, None, j * d:(j + 1) * d]
            if repeat is not None:
                mj = jnp.broadcast_to(mj, (n, repeat, d)).reshape(1, n * repeat, d)
            out.append(mj)
        return out

    g_vec = lambda i, j: norm_g[i, j].reshape(1, d)
    pool_w_bf = pool_w[0].astype(BF16)
    pool_ps = pool_scale[0].reshape(1, d)
    win_bf = ffn_w_in.astype(BF16)
    wout_bf = ffn_w_out.astype(BF16)
    wo_bf = attn_w_o[0].astype(BF16)
    in_cols = attn_w_in.shape[-1]
    w_in_bf = _pad_axis(attn_w_in[0], 1, in_cols + (-in_cols % LANES)).astype(BF16)
    final_vec = final_g.reshape(1, d)

    tr = _row_tile(s, 512)
    sh, sc, gt = mods(0, 0, 0, b)
    x1, pool_p = _pool_layer(x_prompt, jnp.zeros((b, 16, d), F32), sh, sc, gt, g_vec(0, 0),
                             pool_w_bf, pool_ps, tr=tr, t_valid=tr, pos0=0)
    sh, sc, gt = mods(0, 1, 0, b)
    x2 = _ffn_layer(x1, sh, sc, gt, g_vec(0, 1), win_bf[0], wout_bf[0], tr=tr)
    sh, sc, gta = mods(1, 0, 0, b)
    k_p, v_p, ki_p, wit_p, q_hm, qi_hm, k_hm, vt_p, ki_bf = _dsa_project(
        x2, sh, sc, g_vec(1, 0), w_in_bf, tr=tr)
    att = _dsa_prompt(qi_hm, wit_p, q_hm, ki_bf, k_hm, vt_p)
    sh, sc, gt = mods(1, 1, 0, b)
    y_prompt = _ffn_layer(x2, sh, sc, gt, g_vec(1, 1), win_bf[1], wout_bf[1], tr=tr,
                          attn=(att, gta, wo_bf), final_g=final_vec)

    rs = bd * t_dec
    t_pad = -(-t_dec // SUBLANES) * SUBLANES
    sh, sc, gt = mods(0, 0, b, bd)
    prefix = jnp.pad(state_pool[0], ((0, 0), (16 - POOL_STATE, 0), (0, 0)))
    x1s, pool_s = _pool_layer(_pad_axis(x_sample, 1, t_pad), prefix, sh, sc, gt, g_vec(0, 0),
                              pool_w_bf, pool_ps, tr=t_pad, t_valid=t_dec, pos0=past)
    x1s = x1s[:, :t_dec].reshape(1, rs, d)
    sh, sc, gt = mods(0, 1, b, bd, repeat=t_dec)
    x2s = _ffn_layer(x1s, sh, sc, gt, g_vec(0, 1), win_bf[0], wout_bf[0], tr=rs)
    sh, sc, gta = mods(1, 0, b, bd, repeat=t_dec)
    k_s, v_s, ki_s, wit_s, q_s, qi_s, _, _, _ = _dsa_project(x2s, sh, sc, g_vec(1, 0), w_in_bf, tr=rs)

    qi_rows = qi_s[0].reshape(N_IDX_HEADS, bd, t_dec, IDX_DIM).transpose(1, 2, 0, 3)
    qi_rows = qi_rows.reshape(bd, t_dec * N_IDX_HEADS, IDX_DIM)
    wi_rows = wit_s[0].reshape(N_IDX_HEADS, bd, t_dec).transpose(1, 2, 0)
    wi_rows = wi_rows.reshape(bd, t_dec * N_IDX_HEADS, 1)
    q_rows = q_s[0].reshape(N_KV_HEADS, GQA_GROUP, bd, t_dec, HEAD_DIM).transpose(2, 0, 1, 3, 4)
    eye = jnp.eye(N_KV_HEADS, dtype=BF16)
    q_rows = q_rows[:, :, :, :, None, :] * eye[None, :, None, None, :, None]
    q_rows = q_rows.reshape(bd, N_HEADS * t_dec, kv)
    ki_new_t = _pad_axis(ki_s.reshape(bd, t_dec, IDX_DIM).transpose(0, 2, 1), 2, LANES).astype(BF16)
    new_t = lambda a: _pad_axis(a.reshape(bd, t_dec, kv).transpose(0, 2, 1), 2, LANES).astype(BF16)
    cache_k_t = cache_k[0].transpose(0, 2, 3, 1)
    cache_v_t = cache_v[0].transpose(0, 2, 3, 1)
    cache_ki_t = cache_kidx[0].transpose(0, 2, 1)

    pc = next(p for p in (32, 16, 8, 4, 2, 1) if n_pages % p == 0)
    pc_idx = 2 * pc if n_pages % (2 * pc) == 0 else pc
    i_past, i_new = _sample_scores(page_table, qi_rows, wi_rows, ki_new_t, cache_ki_t,
                                   pc=pc_idx, t_dec=t_dec)
    n_keys = past + LANES
    n_keys_pad = -(-n_keys // SELECT_CHUNK) * SELECT_CHUNK
    scores_t = jnp.concatenate([i_past, i_new], axis=-1).reshape(rs, n_keys).T
    scores_t = _pad_axis(scores_t, 0, n_keys_pad, -jnp.inf)
    n_valid = (past + 1 + jnp.arange(rs, dtype=jnp.int32) % t_dec).astype(F32).reshape(1, rs)
    scores_t, thr = _sample_select(scores_t, n_valid, k_sel=min(TOPK_MAX, (past + t_dec) // 4))
    scores = scores_t[:n_keys].T.reshape(bd, t_dec, n_keys)
    att_s = _sample_attend(page_table, q_rows, thr.reshape(bd, t_dec, 1),
                           scores[:, :, :past], scores[:, :, past:], new_t(k_s), new_t(v_s),
                           cache_k_t, cache_v_t, pc=pc, t_dec=t_dec)
    att_s = att_s.reshape(bd, N_KV_HEADS, GQA_GROUP, t_dec, N_KV_HEADS, HEAD_DIM)
    att_s = jnp.stack([att_s[:, n, :, :, n, :] for n in range(N_KV_HEADS)], axis=1)
    att_s = att_s.transpose(0, 3, 1, 2, 4).reshape(1, rs, N_HEADS * HEAD_DIM).astype(BF16)
    sh, sc, gt = mods(1, 1, b, bd, repeat=t_dec)
    y_sample = _ffn_layer(x2s, sh, sc, gt, g_vec(1, 1), win_bf[1], wout_bf[1], tr=rs,
                          attn=(att_s, gta, wo_bf), final_g=final_vec)

    return (y_prompt,
            y_sample.reshape(bd, t_dec, d),
            pool_p[None],
            k_p.reshape(1, b, s, N_KV_HEADS, HEAD_DIM),
            v_p.reshape(1, b, s, N_KV_HEADS, HEAD_DIM),
            ki_p[None],
            pool_s[None],
            k_s.reshape(1, bd, t_dec, N_KV_HEADS, HEAD_DIM),
            v_s.reshape(1, bd, t_dec, N_KV_HEADS, HEAD_DIM),
            ki_s.reshape(1, bd, t_dec, IDX_DIM))
```

```python
import functools

import jax
import jax.numpy as jnp
from jax import lax
from jax.experimental import pallas as pl
from jax.experimental.pallas import tpu as pltpu

F32 = jnp.float32
BF16 = jnp.bfloat16

N_HEADS = 16
N_KV_HEADS = 4
GQA_GROUP = N_HEADS // N_KV_HEADS
HEAD_DIM = 64
N_IDX_HEADS = 8
IDX_DIM = 64
TOPK_MAX = 256
POOL_WINDOWS = (2, 4, 8, 16)
POOL_STATE = max(POOL_WINDOWS) - 1
RMS_EPS = 1e-6

LANES = 128
SUBLANES = 8
MXU_COLS = 256
VMEM_LIMIT_BYTES = 56 * 1024 * 1024
MASKED_SCORE = -1e30
SOFTMAX_SUM_FLOOR = 2.0 ** -100
DENORMAL_KEYS = 0x007FFFFF
KEY_LOWEST = -2139095040 + DENORMAL_KEYS
KEY_POS_INF = 0x7F800000 - DENORMAL_KEYS
LOG2_E = 1.4426950408889634
SELECT_MAX_ITERS = 40
PROBE_SPAN_KEYS = 4 << 23
PASSES_PER_CHECK = 3
SELECT_CHUNK = 1024
COUNT_ACC_ROWS = 32
POOL_HALO = 24
VT_ROWS = HEAD_DIM + 16
NT_DIMS = (((1,), (1,)), ((), ()))


def _cparams(*semantics):
    return pltpu.CompilerParams(dimension_semantics=semantics,
                                vmem_limit_bytes=VMEM_LIMIT_BYTES)


def _silu(x):
    return x * jax.nn.sigmoid(x)


def _normmod(x, g, sc, sh):
    ms = jnp.mean(x * x, axis=-1, keepdims=True)
    return x * lax.rsqrt(ms + RMS_EPS) * g * (1.0 + sc) + sh


def _resident(shape):
    nd = len(shape)
    return pl.BlockSpec(shape, lambda *_: (0,) * nd)


def _mod_kernel(c_ref, w_ref, b_ref, o_ref):
    a = _silu(c_ref[...]).astype(BF16)
    o_ref[0] = jnp.dot(a, w_ref[0].astype(BF16), preferred_element_type=F32) + b_ref[0]


def _modulation(c_all, ada_w, ada_b):
    n_l, d, d3 = ada_w.shape
    bc = c_all.shape[0]
    tn = d3 // 2
    return pl.pallas_call(
        _mod_kernel,
        grid=(n_l, d3 // tn),
        in_specs=[pl.BlockSpec((bc, d), lambda l, j: (0, 0)),
                  pl.BlockSpec((1, d, tn), lambda l, j: (l, 0, j)),
                  pl.BlockSpec((1, 1, tn), lambda l, j: (l, 0, j))],
        out_specs=pl.BlockSpec((1, bc, tn), lambda l, j: (l, 0, j)),
        out_shape=jax.ShapeDtypeStruct((n_l, bc, d3), F32),
        compiler_params=_cparams("parallel", "parallel"),
        name="modulation",
    )(c_all, ada_w, ada_b)


def _pool_kernel(x_ref, pre_ref, sh_ref, sc_ref, gt_ref, g_ref, w_ref, ps_ref,
                 o_ref, st_ref, e_ref, b1_ref, b2_ref, b3_ref, *, tr, t_valid, pos0):
    t = pl.program_id(1)
    d = x_ref.shape[-1]
    grp = d // len(POOL_WINDOWS)
    top = tr + POOL_HALO

    @pl.when(t == 0)
    def _():
        zeros = jnp.zeros((SUBLANES, d), F32)
        e_ref[0:SUBLANES, :] = zeros
        b1_ref[0:SUBLANES, :] = zeros
        b2_ref[0:SUBLANES, :] = zeros
        e_ref[SUBLANES:POOL_HALO, :] = pre_ref[0]

    x = x_ref[0]
    h = _normmod(x, g_ref[...], sc_ref[0], sh_ref[0])
    e_ref[POOL_HALO:top, :] = h
    b1_ref[SUBLANES:top, :] = e_ref[SUBLANES:top, :] + e_ref[SUBLANES - 1:top - 1, :]
    b2_ref[SUBLANES:top, grp:] = b1_ref[SUBLANES:top, grp:] + b1_ref[SUBLANES - 2:top - 2, grp:]
    b3_ref[SUBLANES:top, 2 * grp:] = (b2_ref[SUBLANES:top, 2 * grp:]
                                       + b2_ref[SUBLANES - 4:top - 4, 2 * grp:])
    sums = (b1_ref[POOL_HALO:top, 0:grp],
            b2_ref[POOL_HALO:top, grp:2 * grp],
            b3_ref[POOL_HALO:top, 2 * grp:3 * grp],
            b3_ref[POOL_HALO:top, 3 * grp:] + b3_ref[POOL_HALO - 8:top - 8, 3 * grp:])

    pos = pos0 + t * tr + lax.broadcasted_iota(jnp.int32, (tr, 1), 0)
    ys = []
    for gi, w in enumerate(POOL_WINDOWS):
        inv_cnt = 1.0 / jnp.minimum(w, pos + 1).astype(F32)
        diff = sums[gi] * inv_cnt - h[:, gi * grp:(gi + 1) * grp]
        ys.append(jnp.dot(diff.astype(BF16), w_ref[gi], preferred_element_type=F32))
    y = jnp.concatenate(ys, axis=-1) * ps_ref[...]
    o_ref[0] = x + gt_ref[0] * y

    st_ref[0] = e_ref[POOL_HALO + t_valid - POOL_STATE:POOL_HALO + t_valid, :]
    e_ref[SUBLANES:POOL_HALO, :] = e_ref[tr + SUBLANES:top, :]


def _pool_layer(x, prefix, sh, sc, gt, g, w_bf, ps, *, tr, t_valid, pos0):
    b, r, d = x.shape
    grp = d // len(POOL_WINDOWS)
    nt = r // tr
    row = pl.BlockSpec((1, tr, d), lambda i, j: (i, j, 0))
    per_b = pl.BlockSpec((1, 1, d), lambda i, j: (i, 0, 0))
    vec = pl.BlockSpec((1, d), lambda i, j: (0, 0))
    rows = tr + POOL_HALO
    return pl.pallas_call(
        functools.partial(_pool_kernel, tr=tr, t_valid=t_valid, pos0=pos0),
        grid=(b, nt),
        in_specs=[row,
                  pl.BlockSpec((1, 16, d), lambda i, j: (i, 0, 0)),
                  per_b, per_b, per_b, vec,
                  _resident((len(POOL_WINDOWS), grp, grp)),
                  vec],
        out_specs=[row, pl.BlockSpec((1, POOL_STATE, d), lambda i, j: (i, 0, 0))],
        out_shape=[jax.ShapeDtypeStruct((b, r, d), F32),
                   jax.ShapeDtypeStruct((b, POOL_STATE, d), F32)],
        scratch_shapes=[pltpu.VMEM((rows, d), F32)] * 4,
        compiler_params=_cparams("parallel", "arbitrary"),
        name="pool_layer",
    )(x, prefix, sh, sc, gt, g, w_bf, ps)


def _ffn_kernel(*refs, has_attn, has_final, bounds):
    it = iter(refs)
    x_ref = next(it)
    if has_attn:
        att_ref, gta_ref, wo_ref = next(it), next(it), next(it)
    sh_ref, sc_ref, gt_ref, g_ref, win_ref, wout_ref = (next(it) for _ in range(6))
    if has_final:
        fg_ref = next(it)
    o_ref = next(it)

    x = x_ref[0]
    if has_attn:
        x = x + gta_ref[0] * jnp.dot(att_ref[0], wo_ref[...], preferred_element_type=F32)
    h = _normmod(x, g_ref[...], sc_ref[0], sh_ref[0]).astype(BF16)
    f = wout_ref.shape[0]
    acc = jnp.zeros(x.shape, F32)
    for lo, hi in zip(bounds[:-1], bounds[1:]):
        gate = jnp.dot(h, win_ref[:, lo:hi], preferred_element_type=F32)
        up = jnp.dot(h, win_ref[:, f + lo:f + hi], preferred_element_type=F32)
        act = (_silu(gate) * up).astype(BF16)
        acc = acc + jnp.dot(act, wout_ref[lo:hi, :], preferred_element_type=F32)
    y = x + gt_ref[0] * acc
    if has_final:
        ms = jnp.mean(y * y, axis=-1, keepdims=True)
        y = y * lax.rsqrt(ms + RMS_EPS) * fg_ref[...]
    o_ref[0] = y


def _mod_spec(m, tr):
    d = m.shape[-1]
    if m.shape[1] == 1:
        return pl.BlockSpec((1, 1, d), lambda i, j: (i, 0, 0))
    return pl.BlockSpec((1, tr, d), lambda i, j: (i, j, 0))


def _ffn_layer(x, sh, sc, gt, g, win_bf, wout_bf, *, tr, attn=None, final_g=None):
    b, r, d = x.shape
    f = wout_bf.shape[0]
    row = pl.BlockSpec((1, tr, d), lambda i, j: (i, j, 0))
    vec = pl.BlockSpec((1, d), lambda i, j: (0, 0))
    args, specs = [x], [row]
    if attn is not None:
        att, gta, wo_bf = attn
        args += [att, gta, wo_bf]
        specs += [pl.BlockSpec((1, tr, att.shape[-1]), lambda i, j: (i, j, 0)),
                  _mod_spec(gta, tr), _resident(wo_bf.shape)]
    args += [sh, sc, gt, g, win_bf, wout_bf]
    specs += [_mod_spec(sh, tr), _mod_spec(sc, tr), _mod_spec(gt, tr), vec,
              _resident(win_bf.shape), _resident(wout_bf.shape)]
    if final_g is not None:
        args.append(final_g)
        specs.append(vec)
    tiles = f // MXU_COLS
    bounds = (0, (tiles + 1) // 2 * MXU_COLS, f) if f % MXU_COLS == 0 and tiles > 1 else (0, f)
    return pl.pallas_call(
        functools.partial(_ffn_kernel, has_attn=attn is not None,
                          has_final=final_g is not None, bounds=bounds),
        grid=(b, r // tr),
        in_specs=specs,
        out_specs=row,
        out_shape=jax.ShapeDtypeStruct((b, r, d), F32),
        compiler_params=_cparams("parallel", "parallel"),
        name="ffn_layer",
    )(*args)


def _proj_kernel(x_ref, sh_ref, sc_ref, g_ref, w_ref,
                 k_ref, v_ref, ki_ref, wit_ref, q_ref, qi_ref, kh_ref, vt_ref, kib_ref):
    tr = x_ref.shape[1]
    kv = N_KV_HEADS * HEAD_DIM
    o_k = N_HEADS * HEAD_DIM
    o_v = o_k + kv
    o_qi = o_v + kv
    o_ki = o_qi + N_IDX_HEADS * IDX_DIM
    h = _normmod(x_ref[0], g_ref[...], sc_ref[0], sh_ref[0]).astype(BF16)
    p = jnp.dot(h, w_ref[...], preferred_element_type=F32)
    k_ref[0] = p[:, o_k:o_v]
    v_ref[0] = p[:, o_v:o_qi]
    ki = p[:, o_ki:o_ki + IDX_DIM]
    ki_ref[0] = ki
    kib_ref[0] = ki.astype(BF16)
    tail_t = p[:, o_ki:o_ki + LANES].T
    wit_ref[0] = tail_t[IDX_DIM:IDX_DIM + N_IDX_HEADS] * (N_IDX_HEADS ** -0.5 * IDX_DIM ** -0.5)
    for hd in range(N_HEADS):
        q_ref[0, hd] = (p[:, hd * HEAD_DIM:(hd + 1) * HEAD_DIM]
                        * (HEAD_DIM ** -0.5 * LOG2_E)).astype(BF16)
    for hd in range(N_IDX_HEADS):
        qi_ref[0, hd] = p[:, o_qi + hd * IDX_DIM:o_qi + (hd + 1) * IDX_DIM].astype(BF16)
    ones_pad = (lax.broadcasted_iota(jnp.int32, (tr, LANES - HEAD_DIM), 1) == 0).astype(F32)
    for n in range(N_KV_HEADS):
        kh_ref[0, n] = p[:, o_k + n * HEAD_DIM:o_k + (n + 1) * HEAD_DIM].astype(BF16)
        vn = p[:, o_v + n * HEAD_DIM:o_v + (n + 1) * HEAD_DIM]
        vt_ref[0, n] = jnp.concatenate([vn, ones_pad], axis=-1).T[0:VT_ROWS].astype(BF16)


def _dsa_project(x, sh, sc, g, w_bf, *, tr):
    b, r, d = x.shape
    kv = N_KV_HEADS * HEAD_DIM
    row = lambda n: pl.BlockSpec((1, tr, n), lambda i, j: (i, j, 0))
    heads = lambda nh, n: pl.BlockSpec((1, nh, tr, n), lambda i, j: (i, 0, j, 0))
    sds = jax.ShapeDtypeStruct
    return pl.pallas_call(
        _proj_kernel,
        grid=(b, r // tr),
        in_specs=[row(d), _mod_spec(sh, tr), _mod_spec(sc, tr),
                  pl.BlockSpec((1, d), lambda i, j: (0, 0)), _resident(w_bf.shape)],
        out_specs=[row(kv), row(kv), row(IDX_DIM),
                   pl.BlockSpec((1, N_IDX_HEADS, tr), lambda i, j: (i, 0, j)),
                   heads(N_HEADS, HEAD_DIM), heads(N_IDX_HEADS, IDX_DIM),
                   heads(N_KV_HEADS, HEAD_DIM),
                   pl.BlockSpec((1, N_KV_HEADS, VT_ROWS, tr), lambda i, j: (i, 0, 0, j)),
                   row(IDX_DIM)],
        out_shape=[sds((b, r, kv), F32), sds((b, r, kv), F32), sds((b, r, IDX_DIM), F32),
                   sds((b, N_IDX_HEADS, r), F32),
                   sds((b, N_HEADS, r, HEAD_DIM), BF16), sds((b, N_IDX_HEADS, r, IDX_DIM), BF16),
                   sds((b, N_KV_HEADS, r, HEAD_DIM), BF16), sds((b, N_KV_HEADS, VT_ROWS, r), BF16),
                   sds((b, r, IDX_DIM), BF16)],
        compiler_params=_cparams("parallel", "parallel"),
        name="dsa_project",
    )(x, sh, sc, g, w_bf)


def _key_to_f32(k):
    k = jnp.where(k > 0, k + DENORMAL_KEYS, jnp.where(k < -1, k - DENORMAL_KEYS, k))
    return lax.bitcast_convert_type(jnp.where(k >= 0, k, k ^ 0x7FFFFFFF), F32)


def _f32_to_key(v):
    b = lax.bitcast_convert_type(v, jnp.int32)
    k = jnp.where(b >= 0, b, b ^ 0x7FFFFFFF)
    return jnp.where(k > DENORMAL_KEYS, k - DENORMAL_KEYS,
                     jnp.where(k < -1 - DENORMAL_KEYS, k + DENORMAL_KEYS,
                               jnp.where(k >= 0, 0, -1)))


def _count(s_ref, n_chunks, rc, pred):
    q = s_ref.shape[1]
    ar = min(rc, COUNT_ACC_ROWS)

    def body(c, acc):
        r0 = pl.multiple_of(c * rc, rc)
        hit = jnp.where(pred(s_ref[pl.ds(r0, rc), :], r0), 1.0, 0.0)
        return acc + jnp.sum(hit.reshape(rc // ar, ar, q), axis=0)

    acc = lax.fori_loop(0, n_chunks, body, jnp.zeros((ar, q), F32))
    return jnp.sum(acc, axis=0, keepdims=True)


def _select_threshold(s_ref, n_chunks, rc, n_valid, k_sel, top=None):
    q = s_ref.shape[1]
    kf = float(k_sel)
    if top is not None:
        top_key = _f32_to_key(top)

    def midpoint(lo, hi):
        return jnp.where(jnp.logical_and(lo == 0, hi > 1), 1, (lo & hi) + ((lo ^ hi) >> 1))

    def cond(st):
        return jnp.logical_and(st[0] < SELECT_MAX_ITERS, st[1] > 0.0)

    def body(st):
        for _ in range(PASSES_PER_CHECK):
            st = one_pass(st)
        _, _, lo, hi, c_lo, _ = st
        still = jnp.logical_and(c_lo > kf, midpoint(lo, hi) != lo)
        return (st[0], jnp.sum(jnp.where(still, 1.0, 0.0))) + st[2:]

    def one_pass(st):
        it, _, lo, hi, c_lo, c_hi = st
        mid = midpoint(lo, hi)
        active = jnp.logical_and(c_lo > kf, mid != lo)
        if top is not None:
            probe = jnp.where(it == 0, top_key, top_key - PROBE_SPAN_KEYS)
            use = jnp.logical_and(it < 2, jnp.logical_and(probe > lo, probe < hi))
            mid = jnp.where(use, probe, mid)
        thr = _key_to_f32(mid)
        c = _count(s_ref, n_chunks, rc, lambda x, r0: x >= thr)
        ge = c >= kf
        up = jnp.logical_and(active, ge)
        dn = jnp.logical_and(active, jnp.logical_not(ge))
        lo = jnp.where(up, mid, lo)
        c_lo = jnp.where(up, c, c_lo)
        hi = jnp.where(dn, mid, hi)
        c_hi = jnp.where(dn, c, c_hi)
        return it + 1, st[1], lo, hi, c_lo, c_hi

    st = lax.while_loop(cond, body,
                        (jnp.int32(0), jnp.sum(jnp.where(n_valid > kf, 1.0, 0.0)),
                         jnp.full((1, q), KEY_LOWEST, jnp.int32), jnp.full((1, q), KEY_POS_INF, jnp.int32),
                         n_valid, jnp.zeros((1, q), F32)))
    _, _, lo, _, c_lo, c_hi = st
    thr = _key_to_f32(lo)
    tied = c_lo > kf
    need = kf - c_hi

    @pl.when(jnp.sum(jnp.where(tied, 1.0, 0.0)) > 0.0)
    def _():
        def key_ids(r0, shape):
            return r0 + lax.broadcasted_iota(jnp.int32, shape, 0)

        def jbody(_, st):
            jlo, jhi = st
            jm = (jlo + jhi) >> 1
            c = _count(s_ref, n_chunks, rc,
                       lambda x, r0: jnp.logical_and(x == thr, key_ids(r0, x.shape) <= jm))
            ok = c >= need
            return jnp.where(ok, jlo, jm), jnp.where(ok, jm, jhi)

        n_it = max(1, (s_ref.shape[0] - 1).bit_length())
        _, jcut = lax.fori_loop(0, n_it, jbody,
                                (jnp.full((1, q), -1, jnp.int32),
                                 jnp.zeros((1, q), jnp.int32) + (n_chunks * rc - 1)))

        def drop(c, carry):
            r0 = pl.multiple_of(c * rc, rc)
            x = s_ref[pl.ds(r0, rc), :]
            kill = jnp.logical_and(jnp.logical_and(tied, x == thr), key_ids(r0, x.shape) > jcut)
            s_ref[pl.ds(r0, rc), :] = jnp.where(kill, -jnp.inf, x)
            return carry

        lax.fori_loop(0, n_chunks, drop, 0)

    return thr


def _dsa_prompt_kernel(qi_ref, wit_ref, q_ref, ki_ref, k_ref, vt_ref, o_ref,
                       s_ref, m_ref, acc_ref, sa_ref, sb_ref, ta_ref, tb_ref, pa_ref, pb_ref,
                       *, tq, tk, ts, fill, k_sel):
    s_len = s_ref.shape[0]
    q0 = pl.program_id(1) * tq
    n_fill = (q0 + tq + fill - 1) // fill
    q_ids = q0 + lax.broadcasted_iota(jnp.int32, (1, tq), 1)
    gq = GQA_GROUP * tq

    def scores(kf, top):
        for sub in range(fill // ts):
            top = score_tile(kf * (fill // ts) + sub, top)
        return top

    def score_tile(kt, top):
        r0 = pl.multiple_of(kt * ts, ts)
        ki = ki_ref[0, pl.ds(r0, ts), :]
        acc = jnp.zeros((ts, tq), F32)
        for j in range(N_IDX_HEADS // 2):
            qi2 = qi_ref[0, 2 * j:2 * j + 2].reshape(2 * tq, IDX_DIM)
            s2 = lax.dot_general(ki, qi2, NT_DIMS, preferred_element_type=F32)
            for u in range(2):
                hd = 2 * j + u
                acc = acc + jnp.maximum(s2[:, u * tq:(u + 1) * tq], 0.0) * wit_ref[0, hd:hd + 1, :]
        k_ids = r0 + lax.broadcasted_iota(jnp.int32, (ts, 1), 0)
        visible = jnp.where(k_ids <= q_ids, acc, -jnp.inf)
        s_ref[pl.ds(r0, ts), :] = visible
        return jnp.maximum(top, jnp.max(visible, axis=0, keepdims=True))

    top = lax.fori_loop(0, n_fill, scores, jnp.full((1, tq), -jnp.inf, F32))

    thr = _select_threshold(s_ref, n_fill, fill, (q_ids + 1).astype(F32), k_sel, top=top)

    def raw_scores(kt, dst_ref, top_ref):
        r0 = pl.multiple_of(jnp.minimum(kt * tk, s_len - tk), tk)
        for n in range(N_KV_HEADS):
            qn = q_ref[0, n * GQA_GROUP:(n + 1) * GQA_GROUP].reshape(gq, HEAD_DIM)
            t = lax.dot_general(k_ref[0, n, pl.ds(r0, tk), :], qn, NT_DIMS,
                                preferred_element_type=F32)
            dst_ref[n] = t
            top_ref[n] = jnp.max(t, axis=0, keepdims=True)

    def pv(kt, p_ref, n):
        r0 = pl.multiple_of(jnp.maximum(kt, 0) * tk, tk)
        return jnp.dot(vt_ref[0, n, :, pl.ds(r0, tk)], p_ref[n], preferred_element_type=F32)

    def step(kt, cur, nxt, p_cur, p_prv, masked_max):
        s_cur, top_cur = cur
        raw_scores(kt + 1, *nxt)
        r0 = pl.multiple_of(kt * tk, tk)
        keep = s_ref[pl.ds(r0, tk), :] >= thr
        if masked_max:
            bias = jnp.where(keep, 0.0, MASKED_SCORE)
        else:
            keep01 = jnp.where(keep, 1.0, 0.0).astype(BF16)
        for n in range(N_KV_HEADS):
            pv_n = pv(kt - 1, p_prv, n)
            for g in range(GQA_GROUP):
                cols = slice(g * tq, (g + 1) * tq)
                s = s_cur[n, :, cols]
                if masked_max:
                    s = s + bias
                m_old = m_ref[n, :, cols]
                tile_top = jnp.max(s, axis=0, keepdims=True) if masked_max else top_cur[n, :, cols]
                m_new = jnp.maximum(m_old, tile_top)
                if masked_max:
                    p_cur[n, :, cols] = jnp.exp2(s - m_new).astype(BF16)
                else:
                    p_cur[n, :, cols] = jnp.exp2((s - m_new).astype(BF16)) * keep01
                acc_ref[n, :, cols] = jnp.exp2(m_old - m_new) * (acc_ref[n, :, cols] + pv_n[:, cols])
                m_ref[n, :, cols] = m_new

    n_pairs = (q0 + tq + 2 * tk - 1) // (2 * tk)

    def attention(masked_max):
        m_ref[...] = jnp.full(m_ref.shape, MASKED_SCORE, F32)
        acc_ref[...] = jnp.zeros(acc_ref.shape, F32)
        pb_ref[...] = jnp.zeros(pb_ref.shape, BF16)
        even, odd = (sa_ref, ta_ref), (sb_ref, tb_ref)
        raw_scores(0, *even)

        def pair(i, carry):
            step(2 * i, even, odd, pa_ref, pb_ref, masked_max)
            step(2 * i + 1, odd, even, pb_ref, pa_ref, masked_max)
            return carry

        lax.fori_loop(0, n_pairs, pair, 0)
        for n in range(N_KV_HEADS):
            acc_ref[n] = acc_ref[n] + pv(2 * n_pairs - 1, pb_ref, n)

    attention(masked_max=False)
    sums = acc_ref[:, HEAD_DIM:HEAD_DIM + 1, :]
    underflowed = jnp.sum(jnp.where(sums >= SOFTMAX_SUM_FLOOR, 0.0, 1.0)) > 0.0

    @pl.when(underflowed)
    def _():
        attention(masked_max=True)

    for n in range(N_KV_HEADS):
        a = acc_ref[n]
        a = a[0:HEAD_DIM] / a[HEAD_DIM:HEAD_DIM + 1]
        o = jnp.concatenate([a, jnp.zeros((LANES - HEAD_DIM, gq), F32)], axis=0).T
        for g in range(GQA_GROUP):
            hd = n * GQA_GROUP + g
            o_ref[0, :, hd * HEAD_DIM:(hd + 1) * HEAD_DIM] = (
                o[g * tq:(g + 1) * tq, 0:HEAD_DIM].astype(o_ref.dtype))


def _dsa_prompt(qi_hm, wit, q_hm, ki_bf, k_hm, vt):
    b, _, s, _ = q_hm.shape
    k_sel = min(TOPK_MAX, s // 4)
    tq = _row_tile(s, 128)
    tk = _row_tile(s, 256)
    ts = _row_tile(s, 512)
    fill = _row_tile(s, SELECT_CHUNK)
    assert fill % (2 * tk) == 0 and fill % ts == 0
    gq = GQA_GROUP * tq
    return pl.pallas_call(
        functools.partial(_dsa_prompt_kernel, tq=tq, tk=tk, ts=ts, fill=fill, k_sel=k_sel),
        grid=(b, s // tq),
        in_specs=[pl.BlockSpec((1, N_IDX_HEADS, tq, IDX_DIM), lambda i, j: (i, 0, j, 0)),
                  pl.BlockSpec((1, N_IDX_HEADS, tq), lambda i, j: (i, 0, j)),
                  pl.BlockSpec((1, N_HEADS, tq, HEAD_DIM), lambda i, j: (i, 0, j, 0)),
                  pl.BlockSpec((1, s, IDX_DIM), lambda i, j: (i, 0, 0)),
                  pl.BlockSpec((1, N_KV_HEADS, s, HEAD_DIM), lambda i, j: (i, 0, 0, 0)),
                  pl.BlockSpec((1, N_KV_HEADS, VT_ROWS, s), lambda i, j: (i, 0, 0, 0))],
        out_specs=pl.BlockSpec((1, tq, N_HEADS * HEAD_DIM), lambda i, j: (i, j, 0)),
        out_shape=jax.ShapeDtypeStruct((b, s, N_HEADS * HEAD_DIM), BF16),
        scratch_shapes=[pltpu.VMEM((s, tq), F32),
                        pltpu.VMEM((N_KV_HEADS, 1, gq), F32),
                        pltpu.VMEM((N_KV_HEADS, VT_ROWS, gq), F32),
                        pltpu.VMEM((N_KV_HEADS, tk, gq), F32),
                        pltpu.VMEM((N_KV_HEADS, tk, gq), F32),
                        pltpu.VMEM((N_KV_HEADS, 1, gq), F32),
                        pltpu.VMEM((N_KV_HEADS, 1, gq), F32),
                        pltpu.VMEM((N_KV_HEADS, tk, gq), BF16),
                        pltpu.VMEM((N_KV_HEADS, tk, gq), BF16)],
        compiler_params=_cparams("parallel", "arbitrary"),
        name="dsa_prompt",
    )(qi_hm, wit, q_hm, ki_bf, k_hm, vt)


def _page_specs(page_shape, pc):
    zeros = (0,) * len(page_shape)

    def spec(j):
        return pl.BlockSpec((1,) + page_shape, lambda b, c, pt: (pt[b, c * pc + j],) + zeros)
    return [spec(j) for j in range(pc)]


def _sample_scores_kernel(pt_ref, qi_ref, wi_ref, kin_ref, *rest, pc, t_dec):
    pages, (ip_ref, in_ref, kb_ref) = rest[:pc], rest[pc:]
    c = pl.program_id(1)
    qi = qi_ref[0]
    wi = wi_ref[0]
    page = pages[0].shape[2]

    def score(ki_t):
        s = jnp.dot(qi, ki_t, preferred_element_type=F32)
        s = jnp.maximum(s, 0.0) * wi
        return jnp.sum(s.reshape(t_dec, N_IDX_HEADS, s.shape[-1]), axis=1)

    for j in range(pc):
        kb_ref[:, j * page:(j + 1) * page] = pages[j][0].astype(BF16)
    ip_ref[0] = score(kb_ref[...])

    @pl.when(c == pl.num_programs(1) - 1)
    def _():
        s = score(kin_ref[0])
        col = lax.broadcasted_iota(jnp.int32, s.shape, 1)
        row = lax.broadcasted_iota(jnp.int32, s.shape, 0)
        in_ref[0] = jnp.where(col <= row, s, -jnp.inf)


def _sample_scores(page_table, qi_s, wi_s, ki_new_t, cache_ki_t, *, pc, t_dec):
    bd, n_pages = page_table.shape
    page = cache_ki_t.shape[2]
    rows = qi_s.shape[1]
    grid_spec = pltpu.PrefetchScalarGridSpec(
        num_scalar_prefetch=1,
        grid=(bd, n_pages // pc),
        in_specs=[pl.BlockSpec((1, rows, IDX_DIM), lambda b, c, pt: (b, 0, 0)),
                  pl.BlockSpec((1, rows, 1), lambda b, c, pt: (b, 0, 0)),
                  pl.BlockSpec((1, IDX_DIM, LANES), lambda b, c, pt: (b, 0, 0))]
                 + _page_specs((IDX_DIM, page), pc),
        out_specs=[pl.BlockSpec((1, t_dec, pc * page), lambda b, c, pt: (b, 0, c)),
                   pl.BlockSpec((1, t_dec, LANES), lambda b, c, pt: (b, 0, 0))],
        scratch_shapes=[pltpu.VMEM((IDX_DIM, pc * page), BF16)],
    )
    return pl.pallas_call(
        functools.partial(_sample_scores_kernel, pc=pc, t_dec=t_dec),
        grid_spec=grid_spec,
        out_shape=[jax.ShapeDtypeStruct((bd, t_dec, n_pages * page), F32),
                   jax.ShapeDtypeStruct((bd, t_dec, LANES), F32)],
        compiler_params=_cparams("parallel", "arbitrary"),
        name="sample_scores",
    )(page_table, qi_s, wi_s, ki_new_t, *([cache_ki_t] * pc))


def _sample_select_kernel(s_ref, nv_ref, o_ref, thr_ref, *, k_sel):
    o_ref[...] = s_ref[...]
    thr_ref[...] = _select_threshold(o_ref, o_ref.shape[0] // SELECT_CHUNK, SELECT_CHUNK,
                                     nv_ref[...], k_sel)


def _sample_select(scores_t, n_valid, *, k_sel):
    r, q = scores_t.shape
    return pl.pallas_call(
        functools.partial(_sample_select_kernel, k_sel=k_sel),
        out_shape=[jax.ShapeDtypeStruct((r, q), F32), jax.ShapeDtypeStruct((1, q), F32)],
        compiler_params=pltpu.CompilerParams(vmem_limit_bytes=VMEM_LIMIT_BYTES),
        name="sample_select",
    )(scores_t, n_valid)


def _sample_attend_kernel(pt_ref, q_ref, thr_ref, ip_ref, in_ref, kn_ref, vn_ref, *rest, pc, t_dec):
    k_pages, v_pages = rest[:pc], rest[pc:2 * pc]
    o_ref, kb_ref, vb_ref, m_ref, l_ref, acc_ref = rest[2 * pc:]
    c = pl.program_id(1)
    page = k_pages[0].shape[3]
    rows, kv = q_ref.shape[1], q_ref.shape[2]

    @pl.when(c == 0)
    def _():
        m_ref[...] = jnp.full(m_ref.shape, MASKED_SCORE, F32)
        l_ref[...] = jnp.zeros(l_ref.shape, F32)
        acc_ref[...] = jnp.zeros(acc_ref.shape, F32)

    def update(k_t, v_t, scores):
        bias = jnp.where(scores >= thr_ref[0], 0.0, MASKED_SCORE)
        if SUBLANES % t_dec == 0:
            bias = jnp.concatenate([bias] * (SUBLANES // t_dec), axis=0)
        bias = jnp.concatenate([bias] * (rows // bias.shape[0]), axis=0)
        s = jnp.dot(q_ref[0], k_t, preferred_element_type=F32) + bias
        m_old = m_ref[...]
        m_new = jnp.maximum(m_old, jnp.max(s, axis=1, keepdims=True))
        p = jnp.exp2(s - m_new).astype(BF16)
        alpha = jnp.exp2(m_old - m_new)
        l_ref[...] = alpha * l_ref[...] + jnp.sum(p.astype(F32), axis=1, keepdims=True)
        acc_ref[...] = alpha * acc_ref[...] + lax.dot_general(p, v_t, NT_DIMS,
                                                              preferred_element_type=F32)
        m_ref[...] = m_new

    for j in range(pc):
        kb_ref[:, j * page:(j + 1) * page] = k_pages[j][0].reshape(kv, page).astype(BF16)
        vb_ref[:, j * page:(j + 1) * page] = v_pages[j][0].reshape(kv, page).astype(BF16)
    update(kb_ref[...], vb_ref[...], ip_ref[0])

    @pl.when(c == pl.num_programs(1) - 1)
    def _():
        update(kn_ref[0], vn_ref[0], in_ref[0])
        o_ref[0] = acc_ref[...] / l_ref[...]


def _sample_attend(page_table, q_s, thr, i_past, i_new, k_new_t, v_new_t, cache_k_t, cache_v_t,
                   *, pc, t_dec):
    bd, n_pages = page_table.shape
    page = cache_k_t.shape[3]
    rows, kv = q_s.shape[1], q_s.shape[2]
    per_b = lambda shape: pl.BlockSpec((1,) + shape, lambda b, c, pt: (b, 0, 0))
    grid_spec = pltpu.PrefetchScalarGridSpec(
        num_scalar_prefetch=1,
        grid=(bd, n_pages // pc),
        in_specs=[per_b((rows, kv)),
                  per_b((t_dec, 1)),
                  pl.BlockSpec((1, t_dec, pc * page), lambda b, c, pt: (b, 0, c)),
                  per_b((t_dec, LANES)),
                  per_b((kv, LANES)),
                  per_b((kv, LANES))]
                 + _page_specs((N_KV_HEADS, HEAD_DIM, page), pc) * 2,
        out_specs=per_b((rows, kv)),
        scratch_shapes=[pltpu.VMEM((kv, pc * page), BF16),
                        pltpu.VMEM((kv, pc * page), BF16),
                        pltpu.VMEM((rows, 1), F32),
                        pltpu.VMEM((rows, 1), F32),
                        pltpu.VMEM((rows, kv), F32)],
    )
    return pl.pallas_call(
        functools.partial(_sample_attend_kernel, pc=pc, t_dec=t_dec),
        grid_spec=grid_spec,
        out_shape=jax.ShapeDtypeStruct((bd, rows, kv), F32),
        compiler_params=_cparams("parallel", "arbitrary"),
        name="sample_attend",
    )(page_table, q_s, thr, i_past, i_new, k_new_t, v_new_t,
      *([cache_k_t] * pc), *([cache_v_t] * pc))


def _row_tile(r, cap):
    tr = min(r, cap)
    assert r % tr == 0, (r, tr)
    return tr


def _pad_axis(a, axis, n, value=0.0):
    pad = [(0, 0)] * a.ndim
    pad[axis] = (0, n - a.shape[axis])
    return jnp.pad(a, pad, constant_values=value)


def kernel(x_prompt, x_sample, c_prompt, c_sample, state_pool, cache_k, cache_v, cache_kidx, page_table,
           ada_w, ada_b, norm_g, pool_w, pool_scale, attn_w_in, attn_w_o, ffn_w_in, ffn_w_out, final_g):
    b, s, d = x_prompt.shape
    bd, t_dec, _ = x_sample.shape
    depth = ada_w.shape[0]
    assert depth == 2 and d == N_HEADS * HEAD_DIM
    n_pages = page_table.shape[1]
    page = cache_k.shape[2]
    past = n_pages * page
    kv = N_KV_HEADS * HEAD_DIM

    bc = -(-(b + bd) // SUBLANES) * SUBLANES
    c_all = _pad_axis(jnp.concatenate([c_prompt, c_sample], axis=0), 0, bc)
    mod = _modulation(c_all, ada_w.reshape(depth * 2, d, 3 * d), ada_b.reshape(depth * 2, 1, 3 * d))

    def mods(layer, sub, lo, n, repeat=None):
        m = mod[layer * 2 + sub, lo:lo + n]
        out = []
        for j in range(3):
            mj = m[:, None, j * d:(j + 1) * d]
            if repeat is not None:
                mj = jnp.broadcast_to(mj, (n, repeat, d)).reshape(1, n * repeat, d)
            out.append(mj)
        return out

    g_vec = lambda i, j: norm_g[i, j].reshape(1, d)
    pool_w_bf = pool_w[0].astype(BF16)
    pool_ps = pool_scale[0].reshape(1, d)
    win_bf = ffn_w_in.astype(BF16)
    wout_bf = ffn_w_out.astype(BF16)
    wo_bf = attn_w_o[0].astype(BF16)
    in_cols = attn_w_in.shape[-1]
    w_in_bf = _pad_axis(attn_w_in[0], 1, in_cols + (-in_cols % LANES)).astype(BF16)
    final_vec = final_g.reshape(1, d)

    tr = _row_tile(s, 512)
    sh, sc, gt = mods(0, 0, 0, b)
    x1, pool_p = _pool_layer(x_prompt, jnp.zeros((b, 16, d), F32), sh, sc, gt, g_vec(0, 0),
                             pool_w_bf, pool_ps, tr=tr, t_valid=tr, pos0=0)
    sh, sc, gt = mods(0, 1, 0, b)
    x2 = _ffn_layer(x1, sh, sc, gt, g_vec(0, 1), win_bf[0], wout_bf[0], tr=tr)
    sh, sc, gta = mods(1, 0, 0, b)
    k_p, v_p, ki_p, wit_p, q_hm, qi_hm, k_hm, vt_p, ki_bf = _dsa_project(
        x2, sh, sc, g_vec(1, 0), w_in_bf, tr=tr)
    att = _dsa_prompt(qi_hm, wit_p, q_hm, ki_bf, k_hm, vt_p)
    sh, sc, gt = mods(1, 1, 0, b)
    y_prompt = _ffn_layer(x2, sh, sc, gt, g_vec(1, 1), win_bf[1], wout_bf[1], tr=tr,
                          attn=(att, gta, wo_bf), final_g=final_vec)

    rs = bd * t_dec
    t_pad = -(-t_dec // SUBLANES) * SUBLANES
    sh, sc, gt = mods(0, 0, b, bd)
    prefix = jnp.pad(state_pool[0], ((0, 0), (16 - POOL_STATE, 0), (0, 0)))
    x1s, pool_s = _pool_layer(_pad_axis(x_sample, 1, t_pad), prefix, sh, sc, gt, g_vec(0, 0),
                              pool_w_bf, pool_ps, tr=t_pad, t_valid=t_dec, pos0=past)
    x1s = x1s[:, :t_dec].reshape(1, rs, d)
    sh, sc, gt = mods(0, 1, b, bd, repeat=t_dec)
    x2s = _ffn_layer(x1s, sh, sc, gt, g_vec(0, 1), win_bf[0], wout_bf[0], tr=rs)
    sh, sc, gta = mods(1, 0, b, bd, repeat=t_dec)
    k_s, v_s, ki_s, wit_s, q_s, qi_s, _, _, _ = _dsa_project(x2s, sh, sc, g_vec(1, 0), w_in_bf, tr=rs)

    qi_rows = qi_s[0].reshape(N_IDX_HEADS, bd, t_dec, IDX_DIM).transpose(1, 2, 0, 3)
    qi_rows = qi_rows.reshape(bd, t_dec * N_IDX_HEADS, IDX_DIM)
    wi_rows = wit_s[0].reshape(N_IDX_HEADS, bd, t_dec).transpose(1, 2, 0)
    wi_rows = wi_rows.reshape(bd, t_dec * N_IDX_HEADS, 1)
    q_rows = q_s[0].reshape(N_KV_HEADS, GQA_GROUP, bd, t_dec, HEAD_DIM).transpose(2, 0, 1, 3, 4)
    eye = jnp.eye(N_KV_HEADS, dtype=BF16)
    q_rows = q_rows[:, :, :, :, None, :] * eye[None, :, None, None, :, None]
    q_rows = q_rows.reshape(bd, N_HEADS * t_dec, kv)
    ki_new_t = _pad_axis(ki_s.reshape(bd, t_dec, IDX_DIM).transpose(0, 2, 1), 2, LANES).astype(BF16)
    new_t = lambda a: _pad_axis(a.reshape(bd, t_dec, kv).transpose(0, 2, 1), 2, LANES).astype(BF16)
    cache_k_t = cache_k[0].transpose(0, 2, 3, 1)
    cache_v_t = cache_v[0].transpose(0, 2, 3, 1)
    cache_ki_t = cache_kidx[0].transpose(0, 2, 1)

    pc = next(p for p in (64, 32, 16, 8, 4, 2, 1) if n_pages % p == 0)
    pc_idx = 2 * pc if n_pages % (2 * pc) == 0 else pc
    i_past, i_new = _sample_scores(page_table, qi_rows, wi_rows, ki_new_t, cache_ki_t,
                                   pc=pc_idx, t_dec=t_dec)
    n_keys = past + LANES
    n_keys_pad = -(-n_keys // SELECT_CHUNK) * SELECT_CHUNK
    scores_t = jnp.concatenate([i_past, i_new], axis=-1).reshape(rs, n_keys).T
    scores_t = _pad_axis(scores_t, 0, n_keys_pad, -jnp.inf)
    n_valid = (past + 1 + jnp.arange(rs, dtype=jnp.int32) % t_dec).astype(F32).reshape(1, rs)
    scores_t, thr = _sample_select(scores_t, n_valid, k_sel=min(TOPK_MAX, (past + t_dec) // 4))
    scores = scores_t[:n_keys].T.reshape(bd, t_dec, n_keys)
    att_s = _sample_attend(page_table, q_rows, thr.reshape(bd, t_dec, 1),
                           scores[:, :, :past], scores[:, :, past:], new_t(k_s), new_t(v_s),
                           cache_k_t, cache_v_t, pc=pc, t_dec=t_dec)
    att_s = att_s.reshape(bd, N_KV_HEADS, GQA_GROUP, t_dec, N_KV_HEADS, HEAD_DIM)
    att_s = jnp.stack([att_s[:, n, :, :, n, :] for n in range(N_KV_HEADS)], axis=1)
    att_s = att_s.transpose(0, 3, 1, 2, 4).reshape(1, rs, N_HEADS * HEAD_DIM).astype(BF16)
    sh, sc, gt = mods(1, 1, b, bd, repeat=t_dec)
    y_sample = _ffn_layer(x2s, sh, sc, gt, g_vec(1, 1), win_bf[1], wout_bf[1], tr=rs,
                          attn=(att_s, gta, wo_bf), final_g=final_vec)

    return (y_prompt,
            y_sample.reshape(bd, t_dec, d),
            pool_p[None],
            k_p.reshape(1, b, s, N_KV_HEADS, HEAD_DIM),
            v_p.reshape(1, b, s, N_KV_HEADS, HEAD_DIM),
            ki_p[None],
            pool_s[None],
            k_s.reshape(1, bd, t_dec, N_KV_HEADS, HEAD_DIM),
            v_s.reshape(1, bd, t_dec, N_KV_HEADS, HEAD_DIM),
            ki_s.reshape(1, bd, t_dec, IDX_DIM))
```

```python
import functools

import jax
import jax.numpy as jnp
from jax import lax
from jax.experimental import pallas as pl
from jax.experimental.pallas import tpu as pltpu

F32 = jnp.float32
BF16 = jnp.bfloat16

N_HEADS = 16
N_KV_HEADS = 4
GQA_GROUP = N_HEADS // N_KV_HEADS
HEAD_DIM = 64
N_IDX_HEADS = 8
IDX_DIM = 64
TOPK_MAX = 256
POOL_WINDOWS = (2, 4, 8, 16)
POOL_STATE = max(POOL_WINDOWS) - 1
RMS_EPS = 1e-6

LANES = 128
SUBLANES = 8
MXU_COLS = 256
VMEM_LIMIT_BYTES = 56 * 1024 * 1024
MASKED_SCORE = -1e30
SOFTMAX_SUM_FLOOR = 2.0 ** -100
DENORMAL_KEYS = 0x007FFFFF
KEY_LOWEST = -2139095040 + DENORMAL_KEYS
KEY_POS_INF = 0x7F800000 - DENORMAL_KEYS
LOG2_E = 1.4426950408889634
SELECT_MAX_ITERS = 40
PROBE_SPAN_KEYS = 4 << 23
PASSES_PER_CHECK = 3
SELECT_CHUNK = 1024
COUNT_ACC_ROWS = 32
POOL_HALO = 24
VT_ROWS = HEAD_DIM + 16
NT_DIMS = (((1,), (1,)), ((), ()))


def _cparams(*semantics):
    return pltpu.CompilerParams(dimension_semantics=semantics,
                                vmem_limit_bytes=VMEM_LIMIT_BYTES)


def _silu(x):
    return x * jax.nn.sigmoid(x)


def _normmod(x, g, sc, sh):
    ms = jnp.mean(x * x, axis=-1, keepdims=True)
    return x * lax.rsqrt(ms + RMS_EPS) * g * (1.0 + sc) + sh


def _resident(shape):
    nd = len(shape)
    return pl.BlockSpec(shape, lambda *_: (0,) * nd)


def _mod_kernel(c_ref, w_ref, b_ref, o_ref):
    a = _silu(c_ref[...]).astype(BF16)
    o_ref[0] = jnp.dot(a, w_ref[0].astype(BF16), preferred_element_type=F32) + b_ref[0]


def _modulation(c_all, ada_w, ada_b):
    n_l, d, d3 = ada_w.shape
    bc = c_all.shape[0]
    tn = d3 // 2
    return pl.pallas_call(
        _mod_kernel,
        grid=(n_l, d3 // tn),
        in_specs=[pl.BlockSpec((bc, d), lambda l, j: (0, 0)),
                  pl.BlockSpec((1, d, tn), lambda l, j: (l, 0, j)),
                  pl.BlockSpec((1, 1, tn), lambda l, j: (l, 0, j))],
        out_specs=pl.BlockSpec((1, bc, tn), lambda l, j: (l, 0, j)),
        out_shape=jax.ShapeDtypeStruct((n_l, bc, d3), F32),
        compiler_params=_cparams("parallel", "parallel"),
        name="modulation",
    )(c_all, ada_w, ada_b)


def _pool_kernel(x_ref, pre_ref, sh_ref, sc_ref, gt_ref, g_ref, w_ref, ps_ref,
                 o_ref, st_ref, e_ref, b1_ref, b2_ref, b3_ref, *, tr, t_valid, pos0):
    t = pl.program_id(1)
    d = x_ref.shape[-1]
    grp = d // len(POOL_WINDOWS)
    top = tr + POOL_HALO

    @pl.when(t == 0)
    def _():
        zeros = jnp.zeros((SUBLANES, d), F32)
        e_ref[0:SUBLANES, :] = zeros
        b1_ref[0:SUBLANES, :] = zeros
        b2_ref[0:SUBLANES, :] = zeros
        e_ref[SUBLANES:POOL_HALO, :] = pre_ref[0]

    x = x_ref[0]
    h = _normmod(x, g_ref[...], sc_ref[0], sh_ref[0])
    e_ref[POOL_HALO:top, :] = h
    b1_ref[SUBLANES:top, :] = e_ref[SUBLANES:top, :] + e_ref[SUBLANES - 1:top - 1, :]
    b2_ref[SUBLANES:top, grp:] = b1_ref[SUBLANES:top, grp:] + b1_ref[SUBLANES - 2:top - 2, grp:]
    b3_ref[SUBLANES:top, 2 * grp:] = (b2_ref[SUBLANES:top, 2 * grp:]
                                       + b2_ref[SUBLANES - 4:top - 4, 2 * grp:])
    sums = (b1_ref[POOL_HALO:top, 0:grp],
            b2_ref[POOL_HALO:top, grp:2 * grp],
            b3_ref[POOL_HALO:top, 2 * grp:3 * grp],
            b3_ref[POOL_HALO:top, 3 * grp:] + b3_ref[POOL_HALO - 8:top - 8, 3 * grp:])

    pos = pos0 + t * tr + lax.broadcasted_iota(jnp.int32, (tr, 1), 0)
    ys = []
    for gi, w in enumerate(POOL_WINDOWS):
        inv_cnt = 1.0 / jnp.minimum(w, pos + 1).astype(F32)
        diff = sums[gi] * inv_cnt - h[:, gi * grp:(gi + 1) * grp]
        ys.append(jnp.dot(diff.astype(BF16), w_ref[gi], preferred_element_type=F32))
    y = jnp.concatenate(ys, axis=-1) * ps_ref[...]
    o_ref[0] = x + gt_ref[0] * y

    st_ref[0] = e_ref[POOL_HALO + t_valid - POOL_STATE:POOL_HALO + t_valid, :]
    e_ref[SUBLANES:POOL_HALO, :] = e_ref[tr + SUBLANES:top, :]


def _pool_layer(x, prefix, sh, sc, gt, g, w_bf, ps, *, tr, t_valid, pos0):
    b, r, d = x.shape
    grp = d // len(POOL_WINDOWS)
    nt = r // tr
    row = pl.BlockSpec((1, tr, d), lambda i, j: (i, j, 0))
    per_b = pl.BlockSpec((1, 1, d), lambda i, j: (i, 0, 0))
    vec = pl.BlockSpec((1, d), lambda i, j: (0, 0))
    rows = tr + POOL_HALO
    return pl.pallas_call(
        functools.partial(_pool_kernel, tr=tr, t_valid=t_valid, pos0=pos0),
        grid=(b, nt),
        in_specs=[row,
                  pl.BlockSpec((1, 16, d), lambda i, j: (i, 0, 0)),
                  per_b, per_b, per_b, vec,
                  _resident((len(POOL_WINDOWS), grp, grp)),
                  vec],
        out_specs=[row, pl.BlockSpec((1, POOL_STATE, d), lambda i, j: (i, 0, 0))],
        out_shape=[jax.ShapeDtypeStruct((b, r, d), F32),
                   jax.ShapeDtypeStruct((b, POOL_STATE, d), F32)],
        scratch_shapes=[pltpu.VMEM((rows, d), F32)] * 4,
        compiler_params=_cparams("parallel", "arbitrary"),
        name="pool_layer",
    )(x, prefix, sh, sc, gt, g, w_bf, ps)


def _ffn_kernel(*refs, has_attn, has_final, bounds):
    it = iter(refs)
    x_ref = next(it)
    if has_attn:
        att_ref, gta_ref, wo_ref = next(it), next(it), next(it)
    sh_ref, sc_ref, gt_ref, g_ref, win_ref, wout_ref = (next(it) for _ in range(6))
    if has_final:
        fg_ref = next(it)
    o_ref = next(it)

    x = x_ref[0]
    if has_attn:
        x = x + gta_ref[0] * jnp.dot(att_ref[0], wo_ref[...], preferred_element_type=F32)
    h = _normmod(x, g_ref[...], sc_ref[0], sh_ref[0]).astype(BF16)
    f = wout_ref.shape[0]
    acc = jnp.zeros(x.shape, F32)
    for lo, hi in zip(bounds[:-1], bounds[1:]):
        gate = jnp.dot(h, win_ref[:, lo:hi], preferred_element_type=F32)
        up = jnp.dot(h, win_ref[:, f + lo:f + hi], preferred_element_type=F32)
        act = (_silu(gate) * up).astype(BF16)
        acc = acc + jnp.dot(act, wout_ref[lo:hi, :], preferred_element_type=F32)
    y = x + gt_ref[0] * acc
    if has_final:
        ms = jnp.mean(y * y, axis=-1, keepdims=True)
        y = y * lax.rsqrt(ms + RMS_EPS) * fg_ref[...]
    o_ref[0] = y


def _mod_spec(m, tr):
    d = m.shape[-1]
    if m.shape[1] == 1:
        return pl.BlockSpec((1, 1, d), lambda i, j: (i, 0, 0))
    return pl.BlockSpec((1, tr, d), lambda i, j: (i, j, 0))


def _ffn_layer(x, sh, sc, gt, g, win_bf, wout_bf, *, tr, attn=None, final_g=None):
    b, r, d = x.shape
    f = wout_bf.shape[0]
    row = pl.BlockSpec((1, tr, d), lambda i, j: (i, j, 0))
    vec = pl.BlockSpec((1, d), lambda i, j: (0, 0))
    args, specs = [x], [row]
    if attn is not None:
        att, gta, wo_bf = attn
        args += [att, gta, wo_bf]
        specs += [pl.BlockSpec((1, tr, att.shape[-1]), lambda i, j: (i, j, 0)),
                  _mod_spec(gta, tr), _resident(wo_bf.shape)]
    args += [sh, sc, gt, g, win_bf, wout_bf]
    specs += [_mod_spec(sh, tr), _mod_spec(sc, tr), _mod_spec(gt, tr), vec,
              _resident(win_bf.shape), _resident(wout_bf.shape)]
    if final_g is not None:
        args.append(final_g)
        specs.append(vec)
    tiles = f // MXU_COLS
    bounds = (0, (tiles + 1) // 2 * MXU_COLS, f) if f % MXU_COLS == 0 and tiles > 1 else (0, f)
    return pl.pallas_call(
        functools.partial(_ffn_kernel, has_attn=attn is not None,
                          has_final=final_g is not None, bounds=bounds),
        grid=(b, r // tr),
        in_specs=specs,
        out_specs=row,
        out_shape=jax.ShapeDtypeStruct((b, r, d), F32),
        compiler_params=_cparams("parallel", "parallel"),
        name="ffn_layer",
    )(*args)


def _proj_kernel(x_ref, sh_ref, sc_ref, g_ref, w_ref,
                 k_ref, v_ref, ki_ref, wit_ref, q_ref, qi_ref, kh_ref, vt_ref, kib_ref):
    tr = x_ref.shape[1]
    kv = N_KV_HEADS * HEAD_DIM
    o_k = N_HEADS * HEAD_DIM
    o_v = o_k + kv
    o_qi = o_v + kv
    o_ki = o_qi + N_IDX_HEADS * IDX_DIM
    h = _normmod(x_ref[0], g_ref[...], sc_ref[0], sh_ref[0]).astype(BF16)
    p = jnp.dot(h, w_ref[...], preferred_element_type=F32)
    for c in range(kv // LANES):
        k_ref[0, c * LANES:(c + 1) * LANES, :] = p[:, o_k + c * LANES:o_k + (c + 1) * LANES].T
        v_ref[0, c * LANES:(c + 1) * LANES, :] = p[:, o_v + c * LANES:o_v + (c + 1) * LANES].T
    kib_ref[0] = p[:, o_ki:o_ki + IDX_DIM].astype(BF16)
    tail_t = p[:, o_ki:o_ki + LANES].T
    ki_ref[0] = tail_t[0:IDX_DIM]
    wit_ref[0] = tail_t[IDX_DIM:IDX_DIM + N_IDX_HEADS] * (N_IDX_HEADS ** -0.5 * IDX_DIM ** -0.5)
    for hd in range(N_HEADS):
        q_ref[0, hd] = (p[:, hd * HEAD_DIM:(hd + 1) * HEAD_DIM]
                        * (HEAD_DIM ** -0.5 * LOG2_E)).astype(BF16)
    for hd in range(N_IDX_HEADS):
        qi_ref[0, hd] = p[:, o_qi + hd * IDX_DIM:o_qi + (hd + 1) * IDX_DIM].astype(BF16)
    ones_pad = (lax.broadcasted_iota(jnp.int32, (tr, LANES - HEAD_DIM), 1) == 0).astype(F32)
    for n in range(N_KV_HEADS):
        kh_ref[0, n] = p[:, o_k + n * HEAD_DIM:o_k + (n + 1) * HEAD_DIM].astype(BF16)
        vn = p[:, o_v + n * HEAD_DIM:o_v + (n + 1) * HEAD_DIM]
        vt_ref[0, n] = jnp.concatenate([vn, ones_pad], axis=-1).T[0:VT_ROWS].astype(BF16)


def _dsa_project(x, sh, sc, g, w_bf, *, tr):
    b, r, d = x.shape
    kv = N_KV_HEADS * HEAD_DIM
    row = lambda n: pl.BlockSpec((1, tr, n), lambda i, j: (i, j, 0))
    cols = lambda n: pl.BlockSpec((1, n, tr), lambda i, j: (i, 0, j))
    heads = lambda nh, n: pl.BlockSpec((1, nh, tr, n), lambda i, j: (i, 0, j, 0))
    sds = jax.ShapeDtypeStruct
    return pl.pallas_call(
        _proj_kernel,
        grid=(b, r // tr),
        in_specs=[row(d), _mod_spec(sh, tr), _mod_spec(sc, tr),
                  pl.BlockSpec((1, d), lambda i, j: (0, 0)), _resident(w_bf.shape)],
        out_specs=[cols(kv), cols(kv), cols(IDX_DIM),
                   pl.BlockSpec((1, N_IDX_HEADS, tr), lambda i, j: (i, 0, j)),
                   heads(N_HEADS, HEAD_DIM), heads(N_IDX_HEADS, IDX_DIM),
                   heads(N_KV_HEADS, HEAD_DIM),
                   pl.BlockSpec((1, N_KV_HEADS, VT_ROWS, tr), lambda i, j: (i, 0, 0, j)),
                   row(IDX_DIM)],
        out_shape=[sds((b, kv, r), F32), sds((b, kv, r), F32), sds((b, IDX_DIM, r), F32),
                   sds((b, N_IDX_HEADS, r), F32),
                   sds((b, N_HEADS, r, HEAD_DIM), BF16), sds((b, N_IDX_HEADS, r, IDX_DIM), BF16),
                   sds((b, N_KV_HEADS, r, HEAD_DIM), BF16), sds((b, N_KV_HEADS, VT_ROWS, r), BF16),
                   sds((b, r, IDX_DIM), BF16)],
        compiler_params=_cparams("parallel", "parallel"),
        name="dsa_project",
    )(x, sh, sc, g, w_bf)


def _key_to_f32(k):
    k = jnp.where(k > 0, k + DENORMAL_KEYS, jnp.where(k < -1, k - DENORMAL_KEYS, k))
    return lax.bitcast_convert_type(jnp.where(k >= 0, k, k ^ 0x7FFFFFFF), F32)


def _f32_to_key(v):
    b = lax.bitcast_convert_type(v, jnp.int32)
    k = jnp.where(b >= 0, b, b ^ 0x7FFFFFFF)
    return jnp.where(k > DENORMAL_KEYS, k - DENORMAL_KEYS,
                     jnp.where(k < -1 - DENORMAL_KEYS, k + DENORMAL_KEYS,
                               jnp.where(k >= 0, 0, -1)))


def _count(s_ref, n_chunks, rc, pred):
    q = s_ref.shape[1]
    ar = min(rc, COUNT_ACC_ROWS)

    def body(c, acc):
        r0 = pl.multiple_of(c * rc, rc)
        hit = jnp.where(pred(s_ref[pl.ds(r0, rc), :], r0), 1.0, 0.0)
        return acc + jnp.sum(hit.reshape(rc // ar, ar, q), axis=0)

    acc = lax.fori_loop(0, n_chunks, body, jnp.zeros((ar, q), F32))
    return jnp.sum(acc, axis=0, keepdims=True)


def _select_threshold(s_ref, n_chunks, rc, n_valid, k_sel, top=None):
    q = s_ref.shape[1]
    kf = float(k_sel)
    if top is not None:
        top_key = _f32_to_key(top)

    def midpoint(lo, hi):
        return jnp.where(jnp.logical_and(lo == 0, hi > 1), 1, (lo & hi) + ((lo ^ hi) >> 1))

    def cond(st):
        return jnp.logical_and(st[0] < SELECT_MAX_ITERS, st[1] > 0.0)

    def body(st):
        for _ in range(PASSES_PER_CHECK):
            st = one_pass(st)
        _, _, lo, hi, c_lo, _ = st
        still = jnp.logical_and(c_lo > kf, midpoint(lo, hi) != lo)
        return (st[0], jnp.sum(jnp.where(still, 1.0, 0.0))) + st[2:]

    def one_pass(st):
        it, _, lo, hi, c_lo, c_hi = st
        mid = midpoint(lo, hi)
        active = jnp.logical_and(c_lo > kf, mid != lo)
        if top is not None:
            probe = jnp.where(it == 0, top_key, top_key - PROBE_SPAN_KEYS)
            use = jnp.logical_and(it < 2, jnp.logical_and(probe > lo, probe < hi))
            mid = jnp.where(use, probe, mid)
        thr = _key_to_f32(mid)
        c = _count(s_ref, n_chunks, rc, lambda x, r0: x >= thr)
        ge = c >= kf
        up = jnp.logical_and(active, ge)
        dn = jnp.logical_and(active, jnp.logical_not(ge))
        lo = jnp.where(up, mid, lo)
        c_lo = jnp.where(up, c, c_lo)
        hi = jnp.where(dn, mid, hi)
        c_hi = jnp.where(dn, c, c_hi)
        return it + 1, st[1], lo, hi, c_lo, c_hi

    st = lax.while_loop(cond, body,
                        (jnp.int32(0), jnp.sum(jnp.where(n_valid > kf, 1.0, 0.0)),
                         jnp.full((1, q), KEY_LOWEST, jnp.int32), jnp.full((1, q), KEY_POS_INF, jnp.int32),
                         n_valid, jnp.zeros((1, q), F32)))
    _, _, lo, _, c_lo, c_hi = st
    thr = _key_to_f32(lo)
    tied = c_lo > kf
    need = kf - c_hi

    @pl.when(jnp.sum(jnp.where(tied, 1.0, 0.0)) > 0.0)
    def _():
        def key_ids(r0, shape):
            return r0 + lax.broadcasted_iota(jnp.int32, shape, 0)

        def jbody(_, st):
            jlo, jhi = st
            jm = (jlo + jhi) >> 1
            c = _count(s_ref, n_chunks, rc,
                       lambda x, r0: jnp.logical_and(x == thr, key_ids(r0, x.shape) <= jm))
            ok = c >= need
            return jnp.where(ok, jlo, jm), jnp.where(ok, jm, jhi)

        n_it = max(1, (s_ref.shape[0] - 1).bit_length())
        _, jcut = lax.fori_loop(0, n_it, jbody,
                                (jnp.full((1, q), -1, jnp.int32),
                                 jnp.zeros((1, q), jnp.int32) + (n_chunks * rc - 1)))

        def drop(c, carry):
            r0 = pl.multiple_of(c * rc, rc)
            x = s_ref[pl.ds(r0, rc), :]
            kill = jnp.logical_and(jnp.logical_and(tied, x == thr), key_ids(r0, x.shape) > jcut)
            s_ref[pl.ds(r0, rc), :] = jnp.where(kill, -jnp.inf, x)
            return carry

        lax.fori_loop(0, n_chunks, drop, 0)

    return thr


def _dsa_prompt_kernel(qi_ref, wit_ref, q_ref, ki_ref, k_ref, vt_ref, o_ref,
                       s_ref, m_ref, acc_ref, sa_ref, sb_ref, ta_ref, tb_ref, pa_ref, pb_ref,
                       *, tq, tk, ts, fill, k_sel):
    s_len = s_ref.shape[0]
    q0 = pl.program_id(1) * tq
    n_fill = (q0 + tq + fill - 1) // fill
    q_ids = q0 + lax.broadcasted_iota(jnp.int32, (1, tq), 1)
    gq = GQA_GROUP * tq

    def scores(kf, top):
        for sub in range(fill // ts):
            top = score_tile(kf * (fill // ts) + sub, top)
        return top

    def score_tile(kt, top):
        r0 = pl.multiple_of(kt * ts, ts)
        ki = ki_ref[0, pl.ds(r0, ts), :]
        acc = jnp.zeros((ts, tq), F32)
        for j in range(N_IDX_HEADS // 2):
            qi2 = qi_ref[0, 2 * j:2 * j + 2].reshape(2 * tq, IDX_DIM)
            s2 = lax.dot_general(ki, qi2, NT_DIMS, preferred_element_type=F32)
            for u in range(2):
                hd = 2 * j + u
                acc = acc + jnp.maximum(s2[:, u * tq:(u + 1) * tq], 0.0) * wit_ref[0, hd:hd + 1, :]
        k_ids = r0 + lax.broadcasted_iota(jnp.int32, (ts, 1), 0)
        visible = jnp.where(k_ids <= q_ids, acc, -jnp.inf)
        s_ref[pl.ds(r0, ts), :] = visible
        return jnp.maximum(top, jnp.max(visible, axis=0, keepdims=True))

    top = lax.fori_loop(0, n_fill, scores, jnp.full((1, tq), -jnp.inf, F32))

    thr = _select_threshold(s_ref, n_fill, fill, (q_ids + 1).astype(F32), k_sel, top=top)

    def raw_scores(kt, dst_ref, top_ref):
        r0 = pl.multiple_of(jnp.minimum(kt * tk, s_len - tk), tk)
        for n in range(N_KV_HEADS):
            qn = q_ref[0, n * GQA_GROUP:(n + 1) * GQA_GROUP].reshape(gq, HEAD_DIM)
            t = lax.dot_general(k_ref[0, n, pl.ds(r0, tk), :], qn, NT_DIMS,
                                preferred_element_type=F32)
            dst_ref[n] = t
            top_ref[n] = jnp.max(t, axis=0, keepdims=True)

    def pv(kt, p_ref, n):
        r0 = pl.multiple_of(jnp.maximum(kt, 0) * tk, tk)
        return jnp.dot(vt_ref[0, n, :, pl.ds(r0, tk)], p_ref[n], preferred_element_type=F32)

    def step(kt, cur, nxt, p_cur, p_prv, masked_max):
        s_cur, top_cur = cur
        raw_scores(kt + 1, *nxt)
        r0 = pl.multiple_of(kt * tk, tk)
        keep = s_ref[pl.ds(r0, tk), :] >= thr
        if masked_max:
            bias = jnp.where(keep, 0.0, MASKED_SCORE)
        else:
            keep01 = jnp.where(keep, 1.0, 0.0).astype(BF16)
        for n in range(N_KV_HEADS):
            pv_n = pv(kt - 1, p_prv, n)
            for g in range(GQA_GROUP):
                cols = slice(g * tq, (g + 1) * tq)
                s = s_cur[n, :, cols]
                if masked_max:
                    s = s + bias
                m_old = m_ref[n, :, cols]
                tile_top = jnp.max(s, axis=0, keepdims=True) if masked_max else top_cur[n, :, cols]
                m_new = jnp.maximum(m_old, tile_top)
                if masked_max:
                    p_cur[n, :, cols] = jnp.exp2(s - m_new).astype(BF16)
                else:
                    p_cur[n, :, cols] = jnp.exp2((s - m_new).astype(BF16)) * keep01
                acc_ref[n, :, cols] = jnp.exp2(m_old - m_new) * (acc_ref[n, :, cols] + pv_n[:, cols])
                m_ref[n, :, cols] = m_new

    n_pairs = (q0 + tq + 2 * tk - 1) // (2 * tk)

    def attention(masked_max):
        m_ref[...] = jnp.full(m_ref.shape, MASKED_SCORE, F32)
        acc_ref[...] = jnp.zeros(acc_ref.shape, F32)
        pb_ref[...] = jnp.zeros(pb_ref.shape, BF16)
        even, odd = (sa_ref, ta_ref), (sb_ref, tb_ref)
        raw_scores(0, *even)

        def pair(i, carry):
            step(2 * i, even, odd, pa_ref, pb_ref, masked_max)
            step(2 * i + 1, odd, even, pb_ref, pa_ref, masked_max)
            return carry

        lax.fori_loop(0, n_pairs, pair, 0)
        for n in range(N_KV_HEADS):
            acc_ref[n] = acc_ref[n] + pv(2 * n_pairs - 1, pb_ref, n)

    attention(masked_max=False)
    sums = acc_ref[:, HEAD_DIM:HEAD_DIM + 1, :]
    underflowed = jnp.sum(jnp.where(sums >= SOFTMAX_SUM_FLOOR, 0.0, 1.0)) > 0.0

    @pl.when(underflowed)
    def _():
        attention(masked_max=True)

    for n in range(N_KV_HEADS):
        a = acc_ref[n]
        a = a[0:HEAD_DIM] / a[HEAD_DIM:HEAD_DIM + 1]
        o = jnp.concatenate([a, jnp.zeros((LANES - HEAD_DIM, gq), F32)], axis=0).T
        for g in range(GQA_GROUP):
            hd = n * GQA_GROUP + g
            o_ref[0, :, hd * HEAD_DIM:(hd + 1) * HEAD_DIM] = (
                o[g * tq:(g + 1) * tq, 0:HEAD_DIM].astype(o_ref.dtype))


def _dsa_prompt(qi_hm, wit, q_hm, ki_bf, k_hm, vt):
    b, _, s, _ = q_hm.shape
    k_sel = min(TOPK_MAX, s // 4)
    tq = _row_tile(s, 128)
    tk = _row_tile(s, 256)
    ts = _row_tile(s, 512)
    fill = _row_tile(s, SELECT_CHUNK)
    assert fill % (2 * tk) == 0 and fill % ts == 0
    gq = GQA_GROUP * tq
    return pl.pallas_call(
        functools.partial(_dsa_prompt_kernel, tq=tq, tk=tk, ts=ts, fill=fill, k_sel=k_sel),
        grid=(b, s // tq),
        in_specs=[pl.BlockSpec((1, N_IDX_HEADS, tq, IDX_DIM), lambda i, j: (i, 0, j, 0)),
                  pl.BlockSpec((1, N_IDX_HEADS, tq), lambda i, j: (i, 0, j)),
                  pl.BlockSpec((1, N_HEADS, tq, HEAD_DIM), lambda i, j: (i, 0, j, 0)),
                  pl.BlockSpec((1, s, IDX_DIM), lambda i, j: (i, 0, 0)),
                  pl.BlockSpec((1, N_KV_HEADS, s, HEAD_DIM), lambda i, j: (i, 0, 0, 0)),
                  pl.BlockSpec((1, N_KV_HEADS, VT_ROWS, s), lambda i, j: (i, 0, 0, 0))],
        out_specs=pl.BlockSpec((1, tq, N_HEADS * HEAD_DIM), lambda i, j: (i, j, 0)),
        out_shape=jax.ShapeDtypeStruct((b, s, N_HEADS * HEAD_DIM), BF16),
        scratch_shapes=[pltpu.VMEM((s, tq), F32),
                        pltpu.VMEM((N_KV_HEADS, 1, gq), F32),
                        pltpu.VMEM((N_KV_HEADS, VT_ROWS, gq), F32),
                        pltpu.VMEM((N_KV_HEADS, tk, gq), F32),
                        pltpu.VMEM((N_KV_HEADS, tk, gq), F32),
                        pltpu.VMEM((N_KV_HEADS, 1, gq), F32),
                        pltpu.VMEM((N_KV_HEADS, 1, gq), F32),
                        pltpu.VMEM((N_KV_HEADS, tk, gq), BF16),
                        pltpu.VMEM((N_KV_HEADS, tk, gq), BF16)],
        compiler_params=_cparams("parallel", "arbitrary"),
        name="dsa_prompt",
    )(qi_hm, wit, q_hm, ki_bf, k_hm, vt)


def _page_specs(page_shape, pc):
    zeros = (0,) * len(page_shape)

    def spec(j):
        return pl.BlockSpec((1,) + page_shape, lambda b, c, pt: (pt[b, c * pc + j],) + zeros)
    return [spec(j) for j in range(pc)]


def _sample_scores_kernel(pt_ref, qi_ref, wi_ref, kin_ref, *rest, pc, t_dec):
    pages, (ip_ref, in_ref, kb_ref) = rest[:pc], rest[pc:]
    c = pl.program_id(1)
    qi = qi_ref[0]
    wi = wi_ref[0]
    page = pages[0].shape[2]

    def score(ki_t):
        s = jnp.dot(qi, ki_t, preferred_element_type=F32)
        s = jnp.maximum(s, 0.0) * wi
        return jnp.sum(s.reshape(t_dec, N_IDX_HEADS, s.shape[-1]), axis=1)

    for j in range(pc):
        kb_ref[:, j * page:(j + 1) * page] = pages[j][0].astype(BF16)
    ip_ref[0] = score(kb_ref[...])

    @pl.when(c == pl.num_programs(1) - 1)
    def _():
        s = score(kin_ref[0])
        col = lax.broadcasted_iota(jnp.int32, s.shape, 1)
        row = lax.broadcasted_iota(jnp.int32, s.shape, 0)
        in_ref[0] = jnp.where(col <= row, s, -jnp.inf)


def _sample_scores(page_table, qi_s, wi_s, ki_new_t, cache_ki_t, *, pc, t_dec):
    bd, n_pages = page_table.shape
    page = cache_ki_t.shape[2]
    rows = qi_s.shape[1]
    grid_spec = pltpu.PrefetchScalarGridSpec(
        num_scalar_prefetch=1,
        grid=(bd, n_pages // pc),
        in_specs=[pl.BlockSpec((1, rows, IDX_DIM), lambda b, c, pt: (b, 0, 0)),
                  pl.BlockSpec((1, rows, 1), lambda b, c, pt: (b, 0, 0)),
                  pl.BlockSpec((1, IDX_DIM, LANES), lambda b, c, pt: (b, 0, 0))]
                 + _page_specs((IDX_DIM, page), pc),
        out_specs=[pl.BlockSpec((1, t_dec, pc * page), lambda b, c, pt: (b, 0, c)),
                   pl.BlockSpec((1, t_dec, LANES), lambda b, c, pt: (b, 0, 0))],
        scratch_shapes=[pltpu.VMEM((IDX_DIM, pc * page), BF16)],
    )
    return pl.pallas_call(
        functools.partial(_sample_scores_kernel, pc=pc, t_dec=t_dec),
        grid_spec=grid_spec,
        out_shape=[jax.ShapeDtypeStruct((bd, t_dec, n_pages * page), F32),
                   jax.ShapeDtypeStruct((bd, t_dec, LANES), F32)],
        compiler_params=_cparams("parallel", "arbitrary"),
        name="sample_scores",
    )(page_table, qi_s, wi_s, ki_new_t, *([cache_ki_t] * pc))


def _sample_select_kernel(s_ref, nv_ref, o_ref, thr_ref, *, k_sel):
    o_ref[...] = s_ref[...]
    thr_ref[...] = _select_threshold(o_ref, o_ref.shape[0] // SELECT_CHUNK, SELECT_CHUNK,
                                     nv_ref[...], k_sel)


def _sample_select(scores_t, n_valid, *, k_sel):
    r, q = scores_t.shape
    return pl.pallas_call(
        functools.partial(_sample_select_kernel, k_sel=k_sel),
        out_shape=[jax.ShapeDtypeStruct((r, q), F32), jax.ShapeDtypeStruct((1, q), F32)],
        compiler_params=pltpu.CompilerParams(vmem_limit_bytes=VMEM_LIMIT_BYTES),
        name="sample_select",
    )(scores_t, n_valid)


def _sample_attend_kernel(pt_ref, q_ref, thr_ref, ip_ref, in_ref, kn_ref, vn_ref, *rest, pc, t_dec):
    k_pages, v_pages = rest[:pc], rest[pc:2 * pc]
    o_ref, kb_ref, vb_ref, m_ref, l_ref, acc_ref = rest[2 * pc:]
    c = pl.program_id(1)
    page = k_pages[0].shape[3]
    rows, kv = q_ref.shape[1], q_ref.shape[2]

    @pl.when(c == 0)
    def _():
        m_ref[...] = jnp.full(m_ref.shape, MASKED_SCORE, F32)
        l_ref[...] = jnp.zeros(l_ref.shape, F32)
        acc_ref[...] = jnp.zeros(acc_ref.shape, F32)

    def update(k_t, v_t, scores):
        bias = jnp.where(scores >= thr_ref[0], 0.0, MASKED_SCORE)
        if SUBLANES % t_dec == 0:
            bias = jnp.concatenate([bias] * (SUBLANES // t_dec), axis=0)
        bias = jnp.concatenate([bias] * (rows // bias.shape[0]), axis=0)
        s = jnp.dot(q_ref[0], k_t, preferred_element_type=F32) + bias
        m_old = m_ref[...]
        m_new = jnp.maximum(m_old, jnp.max(s, axis=1, keepdims=True))
        p = jnp.exp2(s - m_new).astype(BF16)
        alpha = jnp.exp2(m_old - m_new)
        l_ref[...] = alpha * l_ref[...] + jnp.sum(p.astype(F32), axis=1, keepdims=True)
        acc_ref[...] = alpha * acc_ref[...] + lax.dot_general(p, v_t, NT_DIMS,
                                                              preferred_element_type=F32)
        m_ref[...] = m_new

    for j in range(pc):
        kb_ref[:, j * page:(j + 1) * page] = k_pages[j][0].reshape(kv, page).astype(BF16)
        vb_ref[:, j * page:(j + 1) * page] = v_pages[j][0].reshape(kv, page).astype(BF16)
    update(kb_ref[...], vb_ref[...], ip_ref[0])

    @pl.when(c == pl.num_programs(1) - 1)
    def _():
        update(kn_ref[0], vn_ref[0], in_ref[0])
        o_ref[0] = acc_ref[...] / l_ref[...]


def _sample_attend(page_table, q_s, thr, i_past, i_new, k_new_t, v_new_t, cache_k_t, cache_v_t,
                   *, pc, t_dec):
    bd, n_pages = page_table.shape
    page = cache_k_t.shape[3]
    rows, kv = q_s.shape[1], q_s.shape[2]
    per_b = lambda shape: pl.BlockSpec((1,) + shape, lambda b, c, pt: (b, 0, 0))
    grid_spec = pltpu.PrefetchScalarGridSpec(
        num_scalar_prefetch=1,
        grid=(bd, n_pages // pc),
        in_specs=[per_b((rows, kv)),
                  per_b((t_dec, 1)),
                  pl.BlockSpec((1, t_dec, pc * page), lambda b, c, pt: (b, 0, c)),
                  per_b((t_dec, LANES)),
                  per_b((kv, LANES)),
                  per_b((kv, LANES))]
                 + _page_specs((N_KV_HEADS, HEAD_DIM, page), pc) * 2,
        out_specs=per_b((rows, kv)),
        scratch_shapes=[pltpu.VMEM((kv, pc * page), BF16),
                        pltpu.VMEM((kv, pc * page), BF16),
                        pltpu.VMEM((rows, 1), F32),
                        pltpu.VMEM((rows, 1), F32),
                        pltpu.VMEM((rows, kv), F32)],
    )
    return pl.pallas_call(
        functools.partial(_sample_attend_kernel, pc=pc, t_dec=t_dec),
        grid_spec=grid_spec,
        out_shape=jax.ShapeDtypeStruct((bd, rows, kv), F32),
        compiler_params=_cparams("parallel", "arbitrary"),
        name="sample_attend",
    )(page_table, q_s, thr, i_past, i_new, k_new_t, v_new_t,
      *([cache_k_t] * pc), *([cache_v_t] * pc))


def _row_tile(r, cap):
    tr = min(r, cap)
    assert r % tr == 0, (r, tr)
    return tr


def _pad_axis(a, axis, n, value=0.0):
    pad = [(0, 0)] * a.ndim
    pad[axis] = (0, n - a.shape[axis])
    return jnp.pad(a, pad, constant_values=value)


def kernel(x_prompt, x_sample, c_prompt, c_sample, state_pool, cache_k, cache_v, cache_kidx, page_table,
           ada_w, ada_b, norm_g, pool_w, pool_scale, attn_w_in, attn_w_o, ffn_w_in, ffn_w_out, final_g):
    b, s, d = x_prompt.shape
    bd, t_dec, _ = x_sample.shape
    depth = ada_w.shape[0]
    assert depth == 2 and d == N_HEADS * HEAD_DIM
    n_pages = page_table.shape[1]
    page = cache_k.shape[2]
    past = n_pages * page
    kv = N_KV_HEADS * HEAD_DIM

    bc = -(-(b + bd) // SUBLANES) * SUBLANES
    c_all = _pad_axis(jnp.concatenate([c_prompt, c_sample], axis=0), 0, bc)
    mod = _modulation(c_all, ada_w.reshape(depth * 2, d, 3 * d), ada_b.reshape(depth * 2, 1, 3 * d))

    def mods(layer, sub, lo, n, repeat=None):
        m = mod[layer * 2 + sub, lo:lo + n]
        out = []
        for j in range(3):
            mj = m[:, None, j * d:(j + 1) * d]
            if repeat is not None:
                mj = jnp.broadcast_to(mj, (n, repeat, d)).reshape(1, n * repeat, d)
            out.append(mj)
        return out

    g_vec = lambda i, j: norm_g[i, j].reshape(1, d)
    pool_w_bf = pool_w[0].astype(BF16)
    pool_ps = pool_scale[0].reshape(1, d)
    win_bf = ffn_w_in.astype(BF16)
    wout_bf = ffn_w_out.astype(BF16)
    wo_bf = attn_w_o[0].astype(BF16)
    in_cols = attn_w_in.shape[-1]
    w_in_bf = _pad_axis(attn_w_in[0], 1, in_cols + (-in_cols % LANES)).astype(BF16)
    final_vec = final_g.reshape(1, d)

    tr = _row_tile(s, 512)
    sh, sc, gt = mods(0, 0, 0, b)
    x1, pool_p = _pool_layer(x_prompt, jnp.zeros((b, 16, d), F32), sh, sc, gt, g_vec(0, 0),
                             pool_w_bf, pool_ps, tr=tr, t_valid=tr, pos0=0)
    sh, sc, gt = mods(0, 1, 0, b)
    x2 = _ffn_layer(x1, sh, sc, gt, g_vec(0, 1), win_bf[0], wout_bf[0], tr=tr)
    sh, sc, gta = mods(1, 0, 0, b)
    k_p, v_p, ki_p, wit_p, q_hm, qi_hm, k_hm, vt_p, ki_bf = _dsa_project(
        x2, sh, sc, g_vec(1, 0), w_in_bf, tr=tr)
    k_p, v_p, ki_p = (a.transpose(0, 2, 1) for a in (k_p, v_p, ki_p))
    att = _dsa_prompt(qi_hm, wit_p, q_hm, ki_bf, k_hm, vt_p)
    sh, sc, gt = mods(1, 1, 0, b)
    y_prompt = _ffn_layer(x2, sh, sc, gt, g_vec(1, 1), win_bf[1], wout_bf[1], tr=tr,
                          attn=(att, gta, wo_bf), final_g=final_vec)

    rs = bd * t_dec
    t_pad = -(-t_dec // SUBLANES) * SUBLANES
    sh, sc, gt = mods(0, 0, b, bd)
    prefix = jnp.pad(state_pool[0], ((0, 0), (16 - POOL_STATE, 0), (0, 0)))
    x1s, pool_s = _pool_layer(_pad_axis(x_sample, 1, t_pad), prefix, sh, sc, gt, g_vec(0, 0),
                              pool_w_bf, pool_ps, tr=t_pad, t_valid=t_dec, pos0=past)
    x1s = x1s[:, :t_dec].reshape(1, rs, d)
    sh, sc, gt = mods(0, 1, b, bd, repeat=t_dec)
    x2s = _ffn_layer(x1s, sh, sc, gt, g_vec(0, 1), win_bf[0], wout_bf[0], tr=rs)
    sh, sc, gta = mods(1, 0, b, bd, repeat=t_dec)
    k_s, v_s, ki_s, wit_s, q_s, qi_s, _, _, _ = _dsa_project(x2s, sh, sc, g_vec(1, 0), w_in_bf, tr=rs)
    k_s, v_s, ki_s = (a.transpose(0, 2, 1) for a in (k_s, v_s, ki_s))

    qi_rows = qi_s[0].reshape(N_IDX_HEADS, bd, t_dec, IDX_DIM).transpose(1, 2, 0, 3)
    qi_rows = qi_rows.reshape(bd, t_dec * N_IDX_HEADS, IDX_DIM)
    wi_rows = wit_s[0].reshape(N_IDX_HEADS, bd, t_dec).transpose(1, 2, 0)
    wi_rows = wi_rows.reshape(bd, t_dec * N_IDX_HEADS, 1)
    q_rows = q_s[0].reshape(N_KV_HEADS, GQA_GROUP, bd, t_dec, HEAD_DIM).transpose(2, 0, 1, 3, 4)
    eye = jnp.eye(N_KV_HEADS, dtype=BF16)
    q_rows = q_rows[:, :, :, :, None, :] * eye[None, :, None, None, :, None]
    q_rows = q_rows.reshape(bd, N_HEADS * t_dec, kv)
    ki_new_t = _pad_axis(ki_s.reshape(bd, t_dec, IDX_DIM).transpose(0, 2, 1), 2, LANES).astype(BF16)
    new_t = lambda a: _pad_axis(a.reshape(bd, t_dec, kv).transpose(0, 2, 1), 2, LANES).astype(BF16)
    cache_k_t = cache_k[0].transpose(0, 2, 3, 1)
    cache_v_t = cache_v[0].transpose(0, 2, 3, 1)
    cache_ki_t = cache_kidx[0].transpose(0, 2, 1)

    pc = next(p for p in (64, 32, 16, 8, 4, 2, 1) if n_pages % p == 0)
    pc_idx = 2 * pc if n_pages % (2 * pc) == 0 else pc
    i_past, i_new = _sample_scores(page_table, qi_rows, wi_rows, ki_new_t, cache_ki_t,
                                   pc=pc_idx, t_dec=t_dec)
    n_keys = past + LANES
    n_keys_pad = -(-n_keys // SELECT_CHUNK) * SELECT_CHUNK
    scores_t = jnp.concatenate([i_past, i_new], axis=-1).reshape(rs, n_keys).T
    scores_t = _pad_axis(scores_t, 0, n_keys_pad, -jnp.inf)
    n_valid = (past + 1 + jnp.arange(rs, dtype=jnp.int32) % t_dec).astype(F32).reshape(1, rs)
    scores_t, thr = _sample_select(scores_t, n_valid, k_sel=min(TOPK_MAX, (past + t_dec) // 4))
    scores = scores_t[:n_keys].T.reshape(bd, t_dec, n_keys)
    att_s = _sample_attend(page_table, q_rows, thr.reshape(bd, t_dec, 1),
                           scores[:, :, :past], scores[:, :, past:], new_t(k_s), new_t(v_s),
                           cache_k_t, cache_v_t, pc=pc, t_dec=t_dec)
    att_s = att_s.reshape(bd, N_KV_HEADS, GQA_GROUP, t_dec, N_KV_HEADS, HEAD_DIM)
    att_s = jnp.stack([att_s[:, n, :, :, n, :] for n in range(N_KV_HEADS)], axis=1)
    att_s = att_s.transpose(0, 3, 1, 2, 4).reshape(1, rs, N_HEADS * HEAD_DIM).astype(BF16)
    sh, sc, gt = mods(1, 1, b, bd, repeat=t_dec)
    y_sample = _ffn_layer(x2s, sh, sc, gt, g_vec(1, 1), win_bf[1], wout_bf[1], tr=rs,
                          attn=(att_s, gta, wo_bf), final_g=final_vec)

    return (y_prompt,
            y_sample.reshape(bd, t_dec, d),
            pool_p[None],
            k_p.reshape(1, b, s, N_KV_HEADS, HEAD_DIM),
            v_p.reshape(1, b, s, N_KV_HEADS, HEAD_DIM),
            ki_p[None],
            pool_s[None],
            k_s.reshape(1, bd, t_dec, N_KV_HEADS, HEAD_DIM),
            v_s.reshape(1, bd, t_dec, N_KV_HEADS, HEAD_DIM),
            ki_s.reshape(1, bd, t_dec, IDX_DIM))
```
